```python
import math
import jax
import jax.numpy as jnp
from jax import lax
import numpy as np

D_MODEL = 1024
BATCH = 16
SEQ = 4096
DEPTH = 4

GRID_W = 64
CTX_LEN = 256

SSM_WIDTH = D_MODEL // 4
SSM_GROUP = 16
SSM_GROUPS = SSM_WIDTH // SSM_GROUP
SSM_STATE = 64
SSM_DT_MIN = 0.001
SSM_DT_MAX = 0.1

DIFF_HEAD_DIM = 64
DIFF_V_DIM = 2 * DIFF_HEAD_DIM
DIFF_WIDTH = D_MODEL // 2
DIFF_HEADS = DIFF_WIDTH // DIFF_V_DIM
DIFF_QK_WIDTH = DIFF_HEADS * 2 * DIFF_HEAD_DIM
Q_BLOCK = 128

NA_WIDTH = D_MODEL - SSM_WIDTH - DIFF_WIDTH
NA_HEAD_DIM = 64
NA_HEADS = NA_WIDTH // NA_HEAD_DIM
NA_WIN_ROWS = 8
NA_WIN_COLS = 16

MIX_WIDTH = SSM_WIDTH + DIFF_WIDTH + NA_WIDTH
IN_SIZES = (SSM_WIDTH, DIFF_QK_WIDTH, DIFF_QK_WIDTH, DIFF_WIDTH, NA_WIDTH, NA_WIDTH, NA_WIDTH)
IN_WIDTH = sum(IN_SIZES)

ROPE_BASE = 10000.0
ROPE_AXIS_PAIRS = DIFF_HEAD_DIM // 4

D_FF = 2816
N_EXPERTS = 8
TOP_K = 2
D_FF_EXPERT = 2816

NORM_EPS = 1e-6
SUBLN_EPS = 1e-5
NEG_INF = -1e30

kernel_name = 'hybrid_s5_diffattn_natten_moe_dit'

F32 = jnp.float32


def rms_norm(x, g, eps=NORM_EPS):
    xf = x.astype(F32)
    xf = xf * lax.rsqrt(jnp.mean(xf * xf, axis=-1, keepdims=True) + eps)
    return (xf * g.astype(F32)).astype(x.dtype)


def modulate(h, shift, scale):
    return h * (1 + scale) + shift


def adaln(cond, w_mod, b_mod):
    m = jax.nn.silu(cond) @ w_mod + b_mod
    return jnp.split(m, 6, axis=-1)


def in_proj_split(p):
    idx = np.cumsum(IN_SIZES)[:-1].tolist()
    return jnp.split(p, idx, axis=-1)


def axial_rope_tables(n_tokens):
    t = jnp.arange(n_tokens, dtype=jnp.int32)
    row = (t // GRID_W).astype(F32)
    col = (t % GRID_W).astype(F32)
    inv = ROPE_BASE ** (-jnp.arange(ROPE_AXIS_PAIRS, dtype=F32) / ROPE_AXIS_PAIRS)
    ang = jnp.concatenate([row[:, None] * inv, col[:, None] * inv], axis=-1)
    return jnp.cos(ang), jnp.sin(ang)


def apply_axial_rope(t, cos, sin):
    L = t.shape[1]
    n_pairs = cos.shape[-1]
    bshape = (1, L) + (1,) * (t.ndim - 3) + (n_pairs,)
    cs = cos.reshape(bshape)
    sn = sin.reshape(bshape)
    tp = t.astype(F32).reshape(t.shape[:-1] + (n_pairs, 2))
    t0 = tp[..., 0]
    t1 = tp[..., 1]
    out = jnp.stack([t0 * cs - t1 * sn, t0 * sn + t1 * cs], axis=-1)
    return out.reshape(t.shape).astype(t.dtype)


def s5_discretise(lam_re, lam_im, b_re, b_im, log_step):
    lam_re = lam_re.astype(F32)
    lam_im = lam_im.astype(F32)
    dt = jnp.exp(log_step.astype(F32))[:, None]
    mag = jnp.exp(lam_re * dt)
    ab_re = mag * jnp.cos(lam_im * dt)
    ab_im = mag * jnp.sin(lam_im * dt)
    den = lam_re * lam_re + lam_im * lam_im
    n_re = ab_re - 1.0
    n_im = ab_im
    f_re = (n_re * lam_re + n_im * lam_im) / den
    f_im = (n_im * lam_re - n_re * lam_im) / den
    b_re = b_re.astype(F32)
    b_im = b_im.astype(F32)
    bb_re = f_re[..., None] * b_re - f_im[..., None] * b_im
    bb_im = f_re[..., None] * b_im + f_im[..., None] * b_re
    return ab_re, ab_im, bb_re, bb_im


def complex_diag_scan(a_re, a_im, b_re, b_im):
    L = b_re.shape[1]
    a_re = jnp.broadcast_to(a_re, (1, L) + a_re.shape)
    a_im = jnp.broadcast_to(a_im, (1, L) + a_im.shape)

    def combine(e1, e2):
        a1r, a1i, b1r, b1i = e1
        a2r, a2i, b2r, b2i = e2
        return (a2r * a1r - a2i * a1i,
                a2r * a1i + a2i * a1r,
                a2r * b1r - a2i * b1i + b2r,
                a2r * b1i + a2i * b1r + b2i)

    _, _, s_re, s_im = lax.associative_scan(combine, (a_re, a_im, b_re, b_im), axis=1)
    return s_re, s_im


def s5_glu(y, w_glu, b_glu):
    g = jax.nn.gelu(y)
    return g * jax.nn.sigmoid(g @ w_glu + b_glu)


def s5_bidirectional(u_lat, u_ctx, lam_re, lam_im, b_re, b_im, c_re, c_im, log_step,
                     d_skip, w_glu, b_glu, need_ctx):
    Bn, L, W = u_lat.shape
    Lc = u_ctx.shape[1]
    ul = u_lat.astype(F32).reshape(Bn, L, SSM_GROUPS, SSM_GROUP)
    uc = u_ctx.astype(F32).reshape(Bn, Lc, SSM_GROUPS, SSM_GROUP)
    seq_fwd = jnp.concatenate([uc, ul], axis=1)
    seq_bwd = jnp.concatenate([uc[:, ::-1], ul[:, ::-1]], axis=1)
    outs = []
    for direction, seq in enumerate((seq_fwd, seq_bwd)):
        ab_re, ab_im, bb_re, bb_im = s5_discretise(lam_re[direction], lam_im[direction],
                                                   b_re[direction], b_im[direction],
                                                   log_step[direction])
        bu_re = jnp.einsum('blgh,gph->blgp', seq, bb_re)
        bu_im = jnp.einsum('blgh,gph->blgp', seq, bb_im)
        s_re, s_im = complex_diag_scan(ab_re, ab_im, bu_re, bu_im)
        cr = c_re[direction].astype(F32)
        ci = c_im[direction].astype(F32)
        outs.append(jnp.einsum('blgp,ghp->blgh', s_re, cr) - jnp.einsum('blgp,ghp->blgh', s_im, ci))
    y_fwd, y_bwd = outs
    dsk = d_skip.astype(F32).reshape(SSM_GROUPS, SSM_GROUP)
    y_lat = y_fwd[:, Lc:] + y_bwd[:, Lc:][:, ::-1] + dsk * ul
    out_lat = s5_glu(y_lat.reshape(Bn, L, W).astype(u_lat.dtype), w_glu, b_glu)
    out_ctx = None
    if need_ctx:
        y_ctx = y_fwd[:, :Lc] + y_bwd[:, :Lc][:, ::-1] + dsk * uc
        out_ctx = s5_glu(y_ctx.reshape(Bn, Lc, W).astype(u_ctx.dtype), w_glu, b_glu)
    return out_lat, out_ctx


def diff_attend(q, k, v, lam):
    s = jnp.einsum('bqhjd,bkhjd->jbhqk', q, k).astype(F32) * (DIFF_HEAD_DIM ** -0.5)
    w = jax.nn.softmax(s[0], axis=-1) - lam * jax.nn.softmax(s[1], axis=-1)
    return jnp.einsum('bhqk,bkhe->bqhe', w.astype(v.dtype), v)


def diff_head_norm(o, g, lam_init):
    Bn, L = o.shape[:2]
    return (rms_norm(o, g, SUBLN_EPS) * (1.0 - lam_init)).reshape(Bn, L, DIFF_WIDTH)


def diff_attention(q_lat, k_lat, v_lat, q_ctx, k_ctx, v_ctx, lam_q1, lam_k1, lam_q2, lam_k2,
                   subln_g, lam_init, cos, sin, need_ctx):
    Bn, L = q_lat.shape[:2]
    lam = (jnp.exp(jnp.sum(lam_q1.astype(F32) * lam_k1.astype(F32)))
           - jnp.exp(jnp.sum(lam_q2.astype(F32) * lam_k2.astype(F32))) + lam_init)
    q_lat = apply_axial_rope(q_lat, cos, sin)
    k_lat = apply_axial_rope(k_lat, cos, sin)
    k_all = jnp.concatenate([k_lat, k_ctx], axis=1)
    v_all = jnp.concatenate([v_lat, v_ctx], axis=1)
    n_blk = L // Q_BLOCK
    q_blocks = q_lat.reshape(Bn, n_blk, Q_BLOCK, DIFF_HEADS, 2, DIFF_HEAD_DIM).swapaxes(0, 1)
    o = lax.map(lambda qb: diff_attend(qb, k_all, v_all, lam), q_blocks)
    o_lat = o.swapaxes(0, 1).reshape(Bn, L, DIFF_HEADS, DIFF_V_DIM)
    out_lat = diff_head_norm(o_lat, subln_g, lam_init)
    out_ctx = None
    if need_ctx:
        out_ctx = diff_head_norm(diff_attend(q_ctx, k_ctx, v_ctx, lam), subln_g, lam_init)
    return out_lat, out_ctx


def neighbourhood_attention(q_lat, k_lat, v_lat, q_ctx, k_ctx, v_ctx, rpb, need_ctx):
    Bn, L, H, d = q_lat.shape
    rows = L // GRID_W
    wr = min(NA_WIN_ROWS, rows)
    scale = d ** -0.5
    qg = q_lat.reshape(Bn, rows, GRID_W, H, d)
    kg = k_lat.reshape(Bn, rows, GRID_W, H, d)
    vg = v_lat.reshape(Bn, rows, GRID_W, H, d)
    cols = jnp.arange(GRID_W, dtype=jnp.int32)
    cstart = jnp.clip(cols - NA_WIN_COLS // 2, 0, GRID_W - NA_WIN_COLS)
    col_valid = (cols[None, :] >= cstart[:, None]) & (cols[None, :] < cstart[:, None] + NA_WIN_COLS)
    col_idx = jnp.clip(cols[None, :] - cols[:, None] + NA_WIN_COLS - 1, 0, 2 * NA_WIN_COLS - 2)

    def row_block(r):
        start = jnp.clip(r - wr // 2, 0, rows - wr)
        qr = lax.dynamic_index_in_dim(qg, r, axis=1, keepdims=False)
        kr = lax.dynamic_slice_in_dim(kg, start, wr, axis=1)
        vr = lax.dynamic_slice_in_dim(vg, start, wr, axis=1)
        row_idx = start + jnp.arange(wr, dtype=jnp.int32) - r + NA_WIN_ROWS - 1
        bias = rpb[:, row_idx[None, :, None], col_idx[:, None, :]].astype(F32)
        s_loc = jnp.einsum('bqhd,bikhd->bhqik', qr, kr).astype(F32) * scale + bias[None]
        s_loc = jnp.where(col_valid[None, None, :, None, :], s_loc, NEG_INF)
        s_ctx = jnp.einsum('bqhd,bkhd->bhqk', qr, k_ctx).astype(F32) * scale
        s = jnp.concatenate([s_loc.reshape(Bn, H, GRID_W, wr * GRID_W), s_ctx], axis=-1)
        p = jax.nn.softmax(s, axis=-1).astype(vr.dtype)
        p_loc = p[..., :wr * GRID_W].reshape(Bn, H, GRID_W, wr, GRID_W)
        p_ctx = p[..., wr * GRID_W:]
        return (jnp.einsum('bhqik,bikhd->bqhd', p_loc, vr)
                + jnp.einsum('bhqk,bkhd->bqhd', p_ctx, v_ctx))

    o = lax.map(row_block, jnp.arange(rows, dtype=jnp.int32))
    out_lat = o.swapaxes(0, 1).reshape(Bn, L, H * d)
    out_ctx = None
    if need_ctx:
        s = jnp.einsum('bqhd,bkhd->bhqk', q_ctx, k_ctx).astype(F32) * scale
        p = jax.nn.softmax(s, axis=-1).astype(v_ctx.dtype)
        out_ctx = jnp.einsum('bhqk,bkhd->bqhd', p, v_ctx).reshape(Bn, -1, H * d)
    return out_lat, out_ctx


def hybrid_mixer(hx, hc, w_in, lam_re, lam_im, b_re, b_im, c_re, c_im, log_step, d_skip,
                 w_glu, b_glu, lam_q1, lam_k1, lam_q2, lam_k2, subln_g, lam_init, rpb,
                 cos, sin, need_ctx):
    Bn, L, _ = hx.shape
    Lc = hc.shape[1]
    u_x, dq_x, dk_x, dv_x, nq_x, nk_x, nv_x = in_proj_split(hx @ w_in)
    u_c, dq_c, dk_c, dv_c, nq_c, nk_c, nv_c = in_proj_split(hc @ w_in)

    ssm_x, ssm_c = s5_bidirectional(u_x, u_c, lam_re, lam_im, b_re, b_im, c_re, c_im,
                                    log_step, d_skip, w_glu, b_glu, need_ctx)

    def qk(t, n):
        return t.reshape(Bn, n, DIFF_HEADS, 2, DIFF_HEAD_DIM)

    diff_x, diff_c = diff_attention(
        qk(dq_x, L), qk(dk_x, L), dv_x.reshape(Bn, L, DIFF_HEADS, DIFF_V_DIM),
        qk(dq_c, Lc), qk(dk_c, Lc), dv_c.reshape(Bn, Lc, DIFF_HEADS, DIFF_V_DIM),
        lam_q1, lam_k1, lam_q2, lam_k2, subln_g, lam_init, cos, sin, need_ctx)

    def nh(t, n):
        return t.reshape(Bn, n, NA_HEADS, NA_HEAD_DIM)

    na_x, na_c = neighbourhood_attention(nh(nq_x, L), nh(nk_x, L), nh(nv_x, L),
                                         nh(nq_c, Lc), nh(nk_c, Lc), nh(nv_c, Lc),
                                         rpb, need_ctx)
    y_lat = jnp.concatenate([ssm_x.astype(hx.dtype), diff_x.astype(hx.dtype), na_x.astype(hx.dtype)], axis=-1)
    y_ctx = None
    if need_ctx:
        y_ctx = jnp.concatenate([ssm_c.astype(hc.dtype), diff_c.astype(hc.dtype), na_c.astype(hc.dtype)], axis=-1)
    return y_lat, y_ctx


def swiglu(h, w1, w3, w2):
    return (jax.nn.silu(h @ w1) * (h @ w3)) @ w2


def moe_swiglu(h, w_router, w1, w3, w2):
    logits = (h @ w_router).astype(F32)
    top_val, top_idx = lax.top_k(logits, TOP_K)
    top_w = jax.nn.softmax(top_val, axis=-1)
    out = jnp.zeros_like(h)
    for e in range(N_EXPERTS):
        gate_e = jnp.sum(jnp.where(top_idx == e, top_w, 0.0), axis=-1)
        out = out + gate_e[..., None].astype(h.dtype) * swiglu(h, w1[e], w3[e], w2[e])
    return out


def channel_mixer(h, layer, ffn_w1, ffn_w3, ffn_w2, moe_router, moe_w1, moe_w3, moe_w2):
    i = layer // 2
    if layer % 2 == 0:
        return swiglu(h, ffn_w1[i], ffn_w3[i], ffn_w2[i])
    return moe_swiglu(h, moe_router[i], moe_w1[i], moe_w3[i], moe_w2[i])


def setup_inputs(seed: int = 0) -> dict:
    key = jax.random.key(seed)
    keys = list(jax.random.split(key, 48))

    def nrm(shape, scale):
        return jax.random.normal(keys.pop(), shape, F32) * scale

    n_dense = (DEPTH + 1) // 2
    n_moe = DEPTH // 2
    G, P, H = SSM_GROUPS, SSM_STATE, SSM_GROUP
    x = nrm((BATCH, SEQ, D_MODEL), 1.0)
    c = nrm((BATCH, D_MODEL), 1.0)
    ctx = nrm((BATCH, CTX_LEN, D_MODEL), 1.0)
    c_ctx = nrm((D_MODEL,), 1.0)
    w_mod = nrm((DEPTH, D_MODEL, 6 * D_MODEL), 0.5 * D_MODEL ** -0.5)
    b_mod = nrm((DEPTH, 6 * D_MODEL), 0.02)
    g_mix = 1.0 + nrm((DEPTH, D_MODEL), 0.02)
    g_ffn = 1.0 + nrm((DEPTH, D_MODEL), 0.02)
    w_in = nrm((DEPTH, D_MODEL, IN_WIDTH), D_MODEL ** -0.5)
    w_out = nrm((DEPTH, MIX_WIDTH, D_MODEL), MIX_WIDTH ** -0.5)
    n_idx = jnp.arange(SSM_STATE, dtype=F32)
    ssm_lam_re = -0.5 + nrm((DEPTH, 2, G, P), 0.01)
    ssm_lam_im = math.pi * n_idx + nrm((DEPTH, 2, G, P), 0.01)
    ssm_b_re = nrm((DEPTH, 2, G, P, H), (2 * H) ** -0.5)
    ssm_b_im = nrm((DEPTH, 2, G, P, H), (2 * H) ** -0.5)
    ssm_c_re = nrm((DEPTH, 2, G, H, P), P ** -0.5)
    ssm_c_im = nrm((DEPTH, 2, G, H, P), P ** -0.5)
    ssm_log_step = jax.random.uniform(keys.pop(), (DEPTH, 2, G), F32,
                                      math.log(SSM_DT_MIN), math.log(SSM_DT_MAX))
    ssm_d = nrm((DEPTH, SSM_WIDTH), 1.0)
    ssm_w_glu = nrm((DEPTH, SSM_WIDTH, SSM_WIDTH), SSM_WIDTH ** -0.5)
    ssm_b_glu = nrm((DEPTH, SSM_WIDTH), 0.02)
    diff_lam_q1 = nrm((DEPTH, DIFF_HEAD_DIM), 0.1)
    diff_lam_k1 = nrm((DEPTH, DIFF_HEAD_DIM), 0.1)
    diff_lam_q2 = nrm((DEPTH, DIFF_HEAD_DIM), 0.1)
    diff_lam_k2 = nrm((DEPTH, DIFF_HEAD_DIM), 0.1)
    diff_subln_g = 1.0 + nrm((DEPTH, DIFF_V_DIM), 0.02)
    na_rpb = nrm((DEPTH, NA_HEADS, 2 * NA_WIN_ROWS - 1, 2 * NA_WIN_COLS - 1), 0.05)
    ffn_w1 = nrm((n_dense, D_MODEL, D_FF), D_MODEL ** -0.5)
    ffn_w3 = nrm((n_dense, D_MODEL, D_FF), D_MODEL ** -0.5)
    ffn_w2 = nrm((n_dense, D_FF, D_MODEL), D_FF ** -0.5)
    moe_router = nrm((n_moe, D_MODEL, N_EXPERTS), D_MODEL ** -0.5)
    moe_w1 = nrm((n_moe, N_EXPERTS, D_MODEL, D_FF_EXPERT), D_MODEL ** -0.5)
    moe_w3 = nrm((n_moe, N_EXPERTS, D_MODEL, D_FF_EXPERT), D_MODEL ** -0.5)
    moe_w2 = nrm((n_moe, N_EXPERTS, D_FF_EXPERT, D_MODEL), D_FF_EXPERT ** -0.5)
    g_final = 1.0 + nrm((D_MODEL,), 0.02)
    return {'x': x, 'c': c, 'ctx': ctx, 'c_ctx': c_ctx, 'w_mod': w_mod, 'b_mod': b_mod,
            'g_mix': g_mix, 'g_ffn': g_ffn, 'w_in': w_in, 'w_out': w_out,
            'ssm_lam_re': ssm_lam_re, 'ssm_lam_im': ssm_lam_im, 'ssm_b_re': ssm_b_re,
            'ssm_b_im': ssm_b_im, 'ssm_c_re': ssm_c_re, 'ssm_c_im': ssm_c_im,
            'ssm_log_step': ssm_log_step, 'ssm_d': ssm_d, 'ssm_w_glu': ssm_w_glu,
            'ssm_b_glu': ssm_b_glu, 'diff_lam_q1': diff_lam_q1, 'diff_lam_k1': diff_lam_k1,
            'diff_lam_q2': diff_lam_q2, 'diff_lam_k2': diff_lam_k2, 'diff_subln_g': diff_subln_g,
            'na_rpb': na_rpb, 'ffn_w1': ffn_w1, 'ffn_w3': ffn_w3, 'ffn_w2': ffn_w2,
            'moe_router': moe_router, 'moe_w1': moe_w1, 'moe_w3': moe_w3, 'moe_w2': moe_w2,
            'g_final': g_final}


def reference(x, c, ctx, c_ctx, w_mod, b_mod, g_mix, g_ffn, w_in, w_out,
              ssm_lam_re, ssm_lam_im, ssm_b_re, ssm_b_im, ssm_c_re, ssm_c_im,
              ssm_log_step, ssm_d, ssm_w_glu, ssm_b_glu,
              diff_lam_q1, diff_lam_k1, diff_lam_q2, diff_lam_k2, diff_subln_g,
              na_rpb, ffn_w1, ffn_w3, ffn_w2, moe_router, moe_w1, moe_w3, moe_w2,
              g_final):
    L = x.shape[1]
    cos, sin = axial_rope_tables(L)
    cx = ctx
    for layer in range(DEPTH):
        need_ctx = layer < DEPTH - 1
        lam_init = 0.8 - 0.6 * math.exp(-0.3 * layer)
        sh1, sc1, gt1, sh2, sc2, gt2 = adaln(c, w_mod[layer], b_mod[layer])
        csh1, csc1, cgt1, csh2, csc2, cgt2 = adaln(c_ctx, w_mod[layer], b_mod[layer])

        hx = modulate(rms_norm(x, g_mix[layer]), sh1[:, None], sc1[:, None])
        hc = modulate(rms_norm(cx, g_mix[layer]), csh1, csc1)
        mx, mc = hybrid_mixer(hx, hc, w_in[layer], ssm_lam_re[layer], ssm_lam_im[layer],
                              ssm_b_re[layer], ssm_b_im[layer], ssm_c_re[layer], ssm_c_im[layer],
                              ssm_log_step[layer], ssm_d[layer], ssm_w_glu[layer], ssm_b_glu[layer],
                              diff_lam_q1[layer], diff_lam_k1[layer], diff_lam_q2[layer],
                              diff_lam_k2[layer], diff_subln_g[layer], lam_init, na_rpb[layer],
                              cos, sin, need_ctx)
        x = x + gt1[:, None] * (mx @ w_out[layer])

        hx = modulate(rms_norm(x, g_ffn[layer]), sh2[:, None], sc2[:, None])
        x = x + gt2[:, None] * channel_mixer(hx, layer, ffn_w1, ffn_w3, ffn_w2,
                                             moe_router, moe_w1, moe_w3, moe_w2)
        if need_ctx:
            cx = cx + cgt1 * (mc @ w_out[layer])
            hc = modulate(rms_norm(cx, g_ffn[layer]), csh2, csc2)
            cx = cx + cgt2 * channel_mixer(hc, layer, ffn_w1, ffn_w3, ffn_w2,
                                           moe_router, moe_w1, moe_w3, moe_w2)
    return rms_norm(x, g_final)
```

```python
import functools
import math

import numpy as np
import jax
import jax.numpy as jnp
from jax import lax
from jax.experimental import pallas as pl
from jax.experimental.pallas import tpu as pltpu

F32 = jnp.float32
BF16 = jnp.bfloat16

GRID_W = 64
SSM_GROUP = 16
SSM_STATE = 64
DIFF_HEAD_DIM = 64
DIFF_V_DIM = 2 * DIFF_HEAD_DIM
NA_HEAD_DIM = 64
NA_WIN_ROWS = 8
NA_WIN_COLS = 16
ROPE_BASE = 10000.0
TOP_K = 2
NORM_EPS = 1e-6
SUBLN_EPS = 1e-5
NEG_INF = -1e30

LANES = 128
SUBLANES = 8
VMEM_LIMIT_BYTES = 56 * 1024 * 1024

S5_CHUNK = 16
NA_BAND_ROWS = 8
NA_KEY_ROWS = 2 * NA_WIN_ROWS

_DN_T = (((1,), (1,)), ((), ()))


def _cparams(*sem):
    return pltpu.CompilerParams(dimension_semantics=sem, vmem_limit_bytes=VMEM_LIMIT_BYTES)


def _dot(a, b):
    return jnp.dot(a, b, preferred_element_type=F32)


def _dot_t(a, b):
    return lax.dot_general(a, b, _DN_T, preferred_element_type=F32)


def _split_bf16(a):
    hi = a.astype(BF16)
    lo = (a - hi.astype(F32)).astype(BF16)
    return hi, lo


def _dot3(a, b):
    a_hi, a_lo = _split_bf16(a)
    b_hi, b_lo = _split_bf16(b)
    return _dot(a_hi, b_hi) + _dot(a_hi, b_lo) + _dot(a_lo, b_hi)


def _norm_mod(x, a, sh):
    ms = jnp.mean(x * x, axis=-1, keepdims=True)
    return (x * lax.rsqrt(ms + NORM_EPS)) * a + sh


def _adaln_kernel(c_ref, w_ref, b_ref, o_ref):
    c = c_ref[...]
    s = c * jax.nn.sigmoid(c)
    o_ref[...] = _dot3(s, w_ref[...]) + b_ref[...]


def _adaln(cond, w_mod, b_mod):
    depth, d, n6 = w_mod.shape
    r = cond.shape[0]
    tn = n6 // 6
    return pl.pallas_call(
        _adaln_kernel,
        grid=(depth, n6 // tn),
        in_specs=[pl.BlockSpec((r, d), lambda l, j: (0, 0)),
                  pl.BlockSpec((None, d, tn), lambda l, j: (l, 0, j)),
                  pl.BlockSpec((None, 1, tn), lambda l, j: (l, 0, j))],
        out_specs=pl.BlockSpec((None, r, tn), lambda l, j: (l, 0, j)),
        out_shape=jax.ShapeDtypeStruct((depth, r, n6), F32),
        compiler_params=_cparams("arbitrary", "arbitrary"),
        name="adaln",
    )(cond, w_mod, b_mod.reshape(depth, 1, n6))


def _in_proj_kernel(x_ref, a_ref, sh_ref, w_ref, cos_ref, sin_ref,
                    u_ref, q_ref, k_ref, v_ref, nq_ref, nk_ref, nv_ref, *, sizes):
    h = _norm_mod(x_ref[...], a_ref[...], sh_ref[...]).astype(BF16)
    p = _dot(h, w_ref[...])
    cs = cos_ref[...]
    sn = sin_ref[...]
    lane = lax.broadcasted_iota(jnp.int32, cs.shape, 1)
    first = (lane % DIFF_HEAD_DIM) < (DIFF_HEAD_DIM // 2)

    def rope(t):
        partner = jnp.where(first, pltpu.roll(t, LANES - DIFF_HEAD_DIM // 2, 1),
                            pltpu.roll(t, DIFF_HEAD_DIM // 2, 1))
        return t * cs + partner * sn

    o = [sum(sizes[:i]) for i in range(len(sizes) + 1)]
    u_ref[...] = p[:, o[0]:o[1]].astype(BF16)
    qk_scale = DIFF_HEAD_DIM ** -0.5
    for c0 in range(0, sizes[1], LANES):
        q_ref[:, c0:c0 + LANES] = (rope(p[:, o[1] + c0:o[1] + c0 + LANES]) * qk_scale).astype(BF16)
        k_ref[:, c0:c0 + LANES] = rope(p[:, o[2] + c0:o[2] + c0 + LANES]).astype(BF16)
    v_ref[...] = p[:, o[3]:o[4]].astype(BF16)
    nq_ref[...] = (p[:, o[4]:o[5]] * (NA_HEAD_DIM ** -0.5)).astype(BF16)
    nk_ref[...] = p[:, o[5]:o[6]].astype(BF16)
    nv_ref[...] = p[:, o[6]:o[7]].astype(BF16)


def _in_proj(x, a_tab, sh_tab, w_in, cos_tab, sin_tab, *, n_lat, seq, sizes, tm):
    n_all, d = x.shape
    n_tiles = n_all // tm
    n_lat_tiles = n_lat // tm
    seq_tiles = seq // tm
    n_mod = a_tab.shape[0]

    def mod_idx(i):
        return (jnp.minimum(i * tm // seq, n_mod - 1), 0, 0)

    def rope_idx(i):
        return (jnp.where(i < n_lat_tiles, i % seq_tiles, seq_tiles), 0)

    row = lambda i: (i, 0)
    return pl.pallas_call(
        functools.partial(_in_proj_kernel, sizes=sizes),
        grid=(n_tiles,),
        in_specs=[pl.BlockSpec((tm, d), row),
                  pl.BlockSpec((None, 1, d), mod_idx),
                  pl.BlockSpec((None, 1, d), mod_idx),
                  pl.BlockSpec(w_in.shape, lambda i: (0, 0)),
                  pl.BlockSpec((tm, LANES), rope_idx),
                  pl.BlockSpec((tm, LANES), rope_idx)],
        out_specs=[pl.BlockSpec((tm, s), row) for s in sizes],
        out_shape=[jax.ShapeDtypeStruct((n_all, s), BF16) for s in sizes],
        compiler_params=_cparams("parallel"),
        name="in_proj",
    )(x, a_tab, sh_tab, w_in, cos_tab, sin_tab)


def _s5_kernel(u_ref, m_ref, e_ref, f_ref, a_ref, d_ref, y_ref, v_scr, s_scr, *, n_ctx_chunks):
    n_chunks, nb, width = u_ref.shape
    rows = n_chunks * nb
    half = width // 2
    u = u_ref[...].reshape(rows, width)
    y = u.astype(F32) * d_ref[...]
    for direction in range(2):
        y = y + _dot(u, m_ref[direction])
        v_scr[...] = _dot(u, e_ref[direction])
        a1 = jnp.broadcast_to(a_ref[direction, 0:1, :], (nb, half))
        a2 = jnp.broadcast_to(a_ref[direction, 1:2, :], (nb, half))
        a2s = jnp.broadcast_to(a_ref[direction, 2:3, :], (nb, half))

        def step(i, carry, direction=direction, a1=a1, a2=a2, a2s=a2s):
            s, ssw = carry
            if direction == 0:
                c = i
            else:
                c = jnp.where(i < n_ctx_chunks, n_ctx_chunks - 1 - i, n_chunks - 1 + n_ctx_chunks - i)
            r0 = pl.multiple_of(c * nb, nb)
            s_scr[pl.ds(r0, nb), :] = s
            v = v_scr[pl.ds(r0, nb), :]
            return (a1 * s + a2 * ssw + v[:, :half], a1 * ssw + a2s * s + v[:, half:])

        zero = jnp.zeros((nb, half), F32)
        lax.fori_loop(0, n_chunks, step, (zero, zero))
        y = y + _dot(s_scr[...].astype(BF16), f_ref[direction])
    y_ref[...] = y.reshape(n_chunks, nb, width).astype(BF16)


def _s5(ut, m, e, f, a, dsk, *, n_ctx_chunks):
    g, n_chunks, nb, width = ut.shape
    rows = n_chunks * nb
    return pl.pallas_call(
        functools.partial(_s5_kernel, n_ctx_chunks=n_ctx_chunks),
        grid=(g,),
        in_specs=[pl.BlockSpec((None, n_chunks, nb, width), lambda i: (i, 0, 0, 0)),
                  pl.BlockSpec((None, 2, width, width), lambda i: (i, 0, 0, 0)),
                  pl.BlockSpec((None, 2, width, width), lambda i: (i, 0, 0, 0)),
                  pl.BlockSpec((None, 2, width // 2, width), lambda i: (i, 0, 0, 0)),
                  pl.BlockSpec((None, 2, SUBLANES, width // 2), lambda i: (i, 0, 0, 0)),
                  pl.BlockSpec((None, 1, width), lambda i: (i, 0, 0))],
        out_specs=pl.BlockSpec((None, n_chunks, nb, width), lambda i: (i, 0, 0, 0)),
        out_shape=jax.ShapeDtypeStruct(ut.shape, BF16),
        scratch_shapes=[pltpu.VMEM((rows, width), F32), pltpu.VMEM((rows, width // 2), F32)],
        compiler_params=_cparams("parallel"),
        name="s5",
    )(ut, m, e, f, a, dsk)


def _s5_matrices(lam_re, lam_im, b_re, b_im, c_re, c_im, log_step, d_skip):
    t = S5_CHUNK
    dt = jnp.exp(log_step.astype(F32))[..., None]
    lr = lam_re.astype(F32)
    li = lam_im.astype(F32)
    mag = jnp.exp(lr * dt)
    ab_re = mag * jnp.cos(li * dt)
    ab_im = mag * jnp.sin(li * dt)
    den = lr * lr + li * li
    n_re = ab_re - 1.0
    n_im = ab_im
    f_re = (n_re * lr + n_im * li) / den
    f_im = (n_im * lr - n_re * li) / den
    br = b_re.astype(F32)
    bi = b_im.astype(F32)
    bb_re = f_re[..., None] * br - f_im[..., None] * bi
    bb_im = f_re[..., None] * bi + f_im[..., None] * br
    j = jnp.arange(t + 1, dtype=F32)[:, None, None, None]
    pw_mag = jnp.exp(lr * dt * j)
    pw_re = pw_mag * jnp.cos(li * dt * j)
    pw_im = pw_mag * jnp.sin(li * dt * j)
    cr = c_re.astype(F32)
    ci = c_im.astype(F32)
    cp_re = cr[None] * pw_re[:, :, :, None, :] - ci[None] * pw_im[:, :, :, None, :]
    cp_im = cr[None] * pw_im[:, :, :, None, :] + ci[None] * pw_re[:, :, :, None, :]
    taps = (jnp.einsum('jdghp,dgpk->jdghk', cp_re[:t], bb_re)
            - jnp.einsum('jdghp,dgpk->jdghk', cp_im[:t], bb_im))
    tau = np.arange(t)[:, None]
    tt = np.arange(t)[None, :]
    mats = []
    for direction, lag in enumerate((tt - tau, tau - tt)):
        valid = lag >= 0
        kk = taps[np.clip(lag, 0, t - 1), direction]
        kk = jnp.where(valid[:, :, None, None, None], kk, 0.0)
        mats.append(jnp.transpose(kk, (2, 0, 4, 1, 3)))
    m = jnp.stack(mats, axis=1)
    gcount = m.shape[0]
    hdim = SSM_GROUP
    m = m.reshape(gcount, 2, t * hdim, t * hdim)
    es = []
    for direction, power in enumerate((t - 1 - np.arange(t), np.arange(t))):
        pr = pw_re[power, direction]
        pi = pw_im[power, direction]
        e_re = pr[:, :, :, None] * bb_re[direction][None] - pi[:, :, :, None] * bb_im[direction][None]
        e_im = pr[:, :, :, None] * bb_im[direction][None] + pi[:, :, :, None] * bb_re[direction][None]
        e_re = jnp.transpose(e_re, (1, 0, 3, 2)).reshape(gcount, t * hdim, -1)
        e_im = jnp.transpose(e_im, (1, 0, 3, 2)).reshape(gcount, t * hdim, -1)
        es.append(jnp.concatenate([e_re, e_im, e_im, e_re], axis=-1))
    e = jnp.stack(es, axis=1)
    fs = []
    for direction, power in enumerate((np.arange(t) + 1, t - np.arange(t))):
        fr = cp_re[power, direction]
        fi = cp_im[power, direction]
        fr = jnp.transpose(fr, (1, 3, 0, 2)).reshape(gcount, -1, t * hdim)
        fi = jnp.transpose(fi, (1, 3, 0, 2)).reshape(gcount, -1, t * hdim)
        fs.append(jnp.concatenate([fr, -fi], axis=1))
    f = jnp.stack(fs, axis=1)
    ar = jnp.transpose(pw_re[t], (1, 0, 2))
    ai = jnp.transpose(pw_im[t], (1, 0, 2))
    rows = jnp.stack([jnp.concatenate([ar, ar], -1), jnp.concatenate([-ai, ai], -1),
                      jnp.concatenate([ai, -ai], -1)], axis=2)
    a = jnp.pad(rows, ((0, 0), (0, 0), (0, SUBLANES - 3), (0, 0)))
    dsk = jnp.tile(d_skip.astype(F32).reshape(gcount, 1, hdim), (1, 1, t))
    return m.astype(BF16), e.astype(BF16), f.astype(BF16), a, dsk


def _diff_kernel(lam_ref, q_ref, *refs, has_lat):
    if has_lat:
        kl_ref, kc_ref, vl_ref, vc_ref, g_ref, o_ref = refs
    else:
        kc_ref, vc_ref, g_ref, o_ref = refs
    q = q_ref[...]
    lane = lax.broadcasted_iota(jnp.int32, q.shape, 1)
    outs = []
    for j in range(2):
        qj = jnp.where((lane >= j * DIFF_HEAD_DIM) & (lane < (j + 1) * DIFF_HEAD_DIM), q, jnp.zeros_like(q))
        sc = _dot_t(qj, kc_ref[...])
        m = jnp.max(sc, axis=-1, keepdims=True)
        if has_lat:
            sl = _dot_t(qj, kl_ref[...])
            m = jnp.maximum(m, jnp.max(sl, axis=-1, keepdims=True))
        pc = jnp.exp(sc - m)
        l = jnp.sum(pc, axis=-1, keepdims=True)
        o = _dot(pc.astype(BF16), vc_ref[...])
        if has_lat:
            pl_ = jnp.exp(sl - m)
            l = l + jnp.sum(pl_, axis=-1, keepdims=True)
            o = o + _dot(pl_.astype(BF16), vl_ref[...])
        outs.append(o / l)
    o = outs[0] - lam_ref[0, 0] * outs[1]
    ms = jnp.mean(o * o, axis=-1, keepdims=True)
    o_ref[...] = (o * lax.rsqrt(ms + SUBLN_EPS) * g_ref[...]).astype(BF16)


def _diff_lat(lam, q, k, v, g, *, batch, seq, n_ctx, tq):
    n_all, width = q.shape
    heads = width // DIFF_V_DIM
    q_tiles = seq // tq
    ctx_blk0 = batch * seq // n_ctx
    smem = pl.BlockSpec(memory_space=pltpu.SMEM)
    qspec = pl.BlockSpec((tq, DIFF_V_DIM), lambda b, h, i: (b * q_tiles + i, h))
    lat = pl.BlockSpec((seq, DIFF_V_DIM), lambda b, h, i: (b, h))
    ctx = pl.BlockSpec((n_ctx, DIFF_V_DIM), lambda b, h, i: (ctx_blk0 + b, h))
    return pl.pallas_call(
        functools.partial(_diff_kernel, has_lat=True),
        grid=(batch, heads, q_tiles),
        in_specs=[smem, qspec, lat, ctx, lat, ctx, pl.BlockSpec((1, DIFF_V_DIM), lambda b, h, i: (0, 0))],
        out_specs=qspec,
        out_shape=jax.ShapeDtypeStruct((n_all, width), BF16),
        compiler_params=_cparams("parallel", "parallel", "arbitrary"),
        name="diff_lat",
    )(lam, q, k, k, v, v, g)


def _diff_ctx_kernel(lam_ref, q_ref, kc_ref, vc_ref, g_ref, prev_ref, o_ref):
    del prev_ref
    _diff_kernel(lam_ref, q_ref, kc_ref, vc_ref, g_ref, o_ref, has_lat=False)


def _diff_ctx(lam, q, k, v, g, out, *, batch, seq, n_ctx):
    n_all, width = q.shape
    heads = width // DIFF_V_DIM
    ctx_blk0 = batch * seq // n_ctx
    smem = pl.BlockSpec(memory_space=pltpu.SMEM)
    ctx = pl.BlockSpec((n_ctx, DIFF_V_DIM), lambda b, h: (ctx_blk0 + b, h))
    return pl.pallas_call(
        _diff_ctx_kernel,
        grid=(batch, heads),
        in_specs=[smem, ctx, ctx, ctx, pl.BlockSpec((1, DIFF_V_DIM), lambda b, h: (0, 0)),
                  pl.BlockSpec(memory_space=pl.ANY)],
        out_specs=ctx,
        out_shape=jax.ShapeDtypeStruct((n_all, width), BF16),
        input_output_aliases={5: 0},
        compiler_params=_cparams("parallel", "parallel"),
        name="diff_ctx",
    )(lam, q, k, v, g, out)


def _na_kernel(q_ref, kl_ref, vl_ref, kc_ref, vc_ref, bias_ref, o_ref, *, grid_rows):
    band = pl.program_id(0)
    key_tokens = NA_KEY_ROWS * GRID_W
    kstart = jnp.clip(band * NA_BAND_ROWS - (NA_KEY_ROWS - NA_BAND_ROWS) // 2, 0, grid_rows - NA_KEY_ROWS)
    off = pl.multiple_of(kstart * GRID_W, (NA_KEY_ROWS - NA_BAND_ROWS) // 2 * GRID_W)
    kw = kl_ref[pl.ds(off, key_tokens), :]
    vw = vl_ref[pl.ds(off, key_tokens), :]
    q = q_ref[...]
    lane = lax.broadcasted_iota(jnp.int32, q.shape, 1)
    acc = jnp.zeros(q.shape, F32)
    for h in range(q.shape[1] // NA_HEAD_DIM):
        mine = (lane >= h * NA_HEAD_DIM) & (lane < (h + 1) * NA_HEAD_DIM)
        qh = jnp.where(mine, q, jnp.zeros_like(q))
        sl = _dot_t(qh, kw) + bias_ref[h]
        sc = _dot_t(qh, kc_ref[...])
        m = jnp.maximum(jnp.max(sl, axis=-1, keepdims=True), jnp.max(sc, axis=-1, keepdims=True))
        pl_ = jnp.exp(sl - m)
        pc = jnp.exp(sc - m)
        l = jnp.sum(pl_, axis=-1, keepdims=True) + jnp.sum(pc, axis=-1, keepdims=True)
        o = (_dot(pl_.astype(BF16), vw) + _dot(pc.astype(BF16), vc_ref[...])) / l
        acc = jnp.where(mine, o, acc)
    o_ref[...] = acc.astype(BF16)


def _na_lat(q, k, v, bias, *, batch, seq, n_ctx):
    n_all, width = q.shape
    grid_rows = seq // GRID_W
    n_bands = grid_rows // NA_BAND_ROWS
    tq = NA_BAND_ROWS * GRID_W
    ctx_blk0 = batch * seq // n_ctx
    heads = width // NA_HEAD_DIM

    def variant(kb, b):
        return (jnp.where(kb == 0, 0, jnp.where(kb == n_bands - 1, 2, 1)), 0, 0, 0)

    qspec = pl.BlockSpec((tq, width), lambda kb, b: (b * n_bands + kb, 0))
    lat = pl.BlockSpec((seq, width), lambda kb, b: (b, 0))
    ctx = pl.BlockSpec((n_ctx, width), lambda kb, b: (ctx_blk0 + b, 0))
    return pl.pallas_call(
        functools.partial(_na_kernel, grid_rows=grid_rows),
        grid=(n_bands, batch),
        in_specs=[qspec, lat, lat, ctx, ctx,
                  pl.BlockSpec((None, heads, tq, NA_KEY_ROWS * GRID_W), variant)],
        out_specs=qspec,
        out_shape=jax.ShapeDtypeStruct((n_all, width), BF16),
        compiler_params=_cparams("arbitrary", "arbitrary"),
        name="na_lat",
    )(q, k, v, k, v, bias)


def _na_bias(rpb, grid_rows):
    n_bands = grid_rows // NA_BAND_ROWS
    qr = np.arange(NA_BAND_ROWS)[:, None]
    kr = np.arange(NA_KEY_ROWS)[None, :]
    qc = np.arange(GRID_W)[:, None]
    kc = np.arange(GRID_W)[None, :]
    cstart = np.clip(qc - NA_WIN_COLS // 2, 0, GRID_W - NA_WIN_COLS)
    col_valid = (kc >= cstart) & (kc < cstart + NA_WIN_COLS)
    col_idx = np.clip(kc - qc + NA_WIN_COLS - 1, 0, 2 * NA_WIN_COLS - 2)
    n_col = 2 * NA_WIN_COLS - 1
    col_onehot = (col_idx[None] == np.arange(n_col)[:, None, None]).astype(np.float32)
    rpb = rpb.astype(F32)
    tables = []
    for band in (0, min(1, n_bands - 1), n_bands - 1):
        r = band * NA_BAND_ROWS + qr
        start = np.clip(r - NA_WIN_ROWS // 2, 0, grid_rows - NA_WIN_ROWS)
        kstart = np.clip(band * NA_BAND_ROWS - (NA_KEY_ROWS - NA_BAND_ROWS) // 2, 0, grid_rows - NA_KEY_ROWS)
        krow = kstart + kr
        row_valid = (krow >= start) & (krow < start + NA_WIN_ROWS)
        row_idx = np.clip(krow - r + NA_WIN_ROWS - 1, 0, 2 * NA_WIN_ROWS - 2)
        by_row = rpb[:, row_idx]
        tab = jnp.einsum('hqkc,cxy->hqxky', by_row, col_onehot, precision=lax.Precision.HIGHEST)
        valid = row_valid[:, None, :, None] & col_valid[None, :, None, :]
        tab = jnp.where(valid[None], tab, NEG_INF)
        tables.append(tab.reshape(tab.shape[0], NA_BAND_ROWS * GRID_W, NA_KEY_ROWS * GRID_W))
    return jnp.stack(tables, axis=0)


def _na_ctx_kernel(q_ref, k_ref, v_ref, prev_ref, o_ref):
    del prev_ref
    q = q_ref[...]
    lane = lax.broadcasted_iota(jnp.int32, q.shape, 1)
    acc = jnp.zeros(q.shape, F32)
    for h in range(q.shape[1] // NA_HEAD_DIM):
        mine = (lane >= h * NA_HEAD_DIM) & (lane < (h + 1) * NA_HEAD_DIM)
        qh = jnp.where(mine, q, jnp.zeros_like(q))
        s = _dot_t(qh, k_ref[...])
        p = jnp.exp(s - jnp.max(s, axis=-1, keepdims=True))
        o = _dot(p.astype(BF16), v_ref[...]) / jnp.sum(p, axis=-1, keepdims=True)
        acc = jnp.where(mine, o, acc)
    o_ref[...] = acc.astype(BF16)


def _na_ctx(q, k, v, out, *, batch, seq, n_ctx):
    n_all, width = q.shape
    ctx_blk0 = batch * seq // n_ctx
    ctx = pl.BlockSpec((n_ctx, width), lambda b: (ctx_blk0 + b, 0))
    return pl.pallas_call(
        _na_ctx_kernel,
        grid=(batch,),
        in_specs=[ctx, ctx, ctx, pl.BlockSpec(memory_space=pl.ANY)],
        out_specs=ctx,
        out_shape=jax.ShapeDtypeStruct((n_all, width), BF16),
        input_output_aliases={3: 0},
        compiler_params=_cparams("parallel"),
        name="na_ctx",
    )(q, k, v, out)


def _gelu_tanh(x):
    return 0.5 * x * (1.0 + jnp.tanh(math.sqrt(2.0 / math.pi) * (x + 0.044715 * (x * x * x))))


def _out_proj_kernel(x_ref, y_ref, df_ref, na_ref, wg_ref, bg_ref, wo_ref, gt_ref, o_ref):
    g = _gelu_tanh(y_ref[...].astype(F32))
    ssm = g * jax.nn.sigmoid(_dot(g.astype(BF16), wg_ref[...]) + bg_ref[...])
    w_ssm = y_ref.shape[1]
    w_diff = df_ref.shape[1]
    acc = _dot(ssm.astype(BF16), wo_ref[0:w_ssm, :])
    acc = acc + _dot(df_ref[...], wo_ref[w_ssm:w_ssm + w_diff, :])
    acc = acc + _dot(na_ref[...], wo_ref[w_ssm + w_diff:, :])
    o_ref[...] = x_ref[...] + gt_ref[...] * acc


def _out_proj(x, y_ssm, diff, na, w_glu, b_glu, w_out, gt_tab, *, n_rows, seq, tm):
    n_all, d = x.shape
    n_mod = gt_tab.shape[0]

    def mod_idx(i):
        return (jnp.minimum(i * tm // seq, n_mod - 1), 0, 0)

    row = lambda i: (i, 0)
    full = lambda i: (0, 0)
    return pl.pallas_call(
        _out_proj_kernel,
        grid=(n_rows // tm,),
        in_specs=[pl.BlockSpec((tm, d), row),
                  pl.BlockSpec((tm, y_ssm.shape[1]), row),
                  pl.BlockSpec((tm, diff.shape[1]), row),
                  pl.BlockSpec((tm, na.shape[1]), row),
                  pl.BlockSpec(w_glu.shape, full),
                  pl.BlockSpec(b_glu.shape, full),
                  pl.BlockSpec(w_out.shape, full),
                  pl.BlockSpec((None, 1, d), mod_idx)],
        out_specs=pl.BlockSpec((tm, d), row),
        out_shape=jax.ShapeDtypeStruct((n_all, d), F32),
        input_output_aliases={0: 0},
        compiler_params=_cparams("parallel"),
        name="out_proj",
    )(x, y_ssm, diff, na, w_glu, b_glu, w_out, gt_tab)


def _swiglu_into(h, w1_ref, w3_ref, w2_ref, acc_ref):
    n_chunks = w1_ref.shape[0]

    def chunk(f, carry):
        a = _dot(h, w1_ref[f])
        b = _dot(h, w3_ref[f])
        act = (a * jax.nn.sigmoid(a) * b).astype(BF16)
        upd = _dot(act, w2_ref[f])

        @pl.when(f == 0)
        def _():
            acc_ref[...] = upd

        @pl.when(f > 0)
        def _():
            acc_ref[...] += upd

        return carry

    lax.fori_loop(0, n_chunks, chunk, 0)


def _ffn_kernel(x_ref, a_ref, sh_ref, gt_ref, w1_ref, w3_ref, w2_ref, o_ref, acc_ref):
    x = x_ref[...]
    h = _norm_mod(x, a_ref[...], sh_ref[...]).astype(BF16)
    _swiglu_into(h, w1_ref, w3_ref, w2_ref, acc_ref)
    o_ref[...] = x + gt_ref[...] * acc_ref[...]


def _resident(shape, index_map):
    return pl.BlockSpec(shape, index_map, pipeline_mode=pl.Buffered(1))


def _ffn(x, a_tab, sh_tab, gt_tab, w1, w3, w2, *, n_rows, seq, tm):
    n_all, d = x.shape
    n_mod = a_tab.shape[0]

    def mod_idx(i):
        return (jnp.minimum(i * tm // seq, n_mod - 1), 0, 0)

    row = lambda i: (i, 0)
    full3 = lambda i: (0, 0, 0)
    mod = pl.BlockSpec((None, 1, d), mod_idx)
    return pl.pallas_call(
        _ffn_kernel,
        grid=(n_rows // tm,),
        in_specs=[pl.BlockSpec((tm, d), row), mod, mod, mod,
                  _resident(w1.shape, full3), _resident(w3.shape, full3), _resident(w2.shape, full3)],
        out_specs=pl.BlockSpec((tm, d), row),
        out_shape=jax.ShapeDtypeStruct((n_all, d), F32),
        scratch_shapes=[pltpu.VMEM((tm, d), F32)],
        input_output_aliases={0: 0},
        compiler_params=_cparams("parallel"),
        name="ffn",
    )(x, a_tab, sh_tab, gt_tab, w1, w3, w2)


def _router_kernel(x_ref, a_ref, sh_ref, wr_ref, h_ref, r_ref):
    h = _norm_mod(x_ref[...], a_ref[...], sh_ref[...])
    h_ref[...] = h.astype(BF16)
    n_exp = 8
    logits = _dot3(h, wr_ref[...])
    lane = lax.broadcasted_iota(jnp.int32, logits.shape, 1)
    logits = jnp.where(lane < n_exp, logits, -jnp.inf)
    m1 = jnp.max(logits, axis=-1, keepdims=True)
    i1 = jnp.min(jnp.where(logits == m1, lane, LANES), axis=-1, keepdims=True)
    rest = jnp.where(lane == i1, -jnp.inf, logits)
    m2 = jnp.max(rest, axis=-1, keepdims=True)
    i2 = jnp.min(jnp.where(rest == m2, lane, LANES), axis=-1, keepdims=True)
    e2 = jnp.exp(m2 - m1)
    g1 = 1.0 / (1.0 + e2)
    g2 = e2 / (1.0 + e2)
    r_ref[...] = jnp.where(lane == 0, i1.astype(F32),
                           jnp.where(lane == 1, i2.astype(F32),
                                     jnp.where(lane == 2, g1, jnp.where(lane == 3, g2, 0.0))))


def _router(x, a_tab, sh_tab, w_router, *, n_rows, seq, tm):
    n_all, d = x.shape
    n_mod = a_tab.shape[0]

    def mod_idx(i):
        return (jnp.minimum(i * tm // seq, n_mod - 1), 0, 0)

    row = lambda i: (i, 0)
    mod = pl.BlockSpec((None, 1, d), mod_idx)
    return pl.pallas_call(
        _router_kernel,
        grid=(n_rows // tm,),
        in_specs=[pl.BlockSpec((tm, d), row), mod, mod, pl.BlockSpec(w_router.shape, lambda i: (0, 0))],
        out_specs=[pl.BlockSpec((tm, d), row), pl.BlockSpec((tm, LANES), row)],
        out_shape=[jax.ShapeDtypeStruct((n_rows, d), BF16), jax.ShapeDtypeStruct((n_rows, LANES), F32)],
        compiler_params=_cparams("parallel"),
        name="router",
    )(x, a_tab, sh_tab, w_router)


def _moe_ffn_kernel(te_ref, tv_ref, h_ref, w1_ref, w3_ref, w2_ref, o_ref, acc_ref):
    i = pl.program_id(0)

    @pl.when(tv_ref[i] > 0)
    def _():
        _swiglu_into(h_ref[...], w1_ref, w3_ref, w2_ref, acc_ref)
        o_ref[...] = acc_ref[...].astype(BF16)

    @pl.when(tv_ref[i] == 0)
    def _():
        o_ref[...] = jnp.zeros(o_ref.shape, BF16)


def _moe_ffn(tile_expert, tile_valid, hs, w1, w3, w2, *, tm):
    n_slots, d = hs.shape
    wspec = lambda w: _resident((None,) + w.shape[1:], lambda i, te, tv: (te[i], 0, 0, 0))
    row = lambda i, te, tv: (i, 0)
    return pl.pallas_call(
        _moe_ffn_kernel,
        grid_spec=pltpu.PrefetchScalarGridSpec(
            num_scalar_prefetch=2,
            grid=(n_slots // tm,),
            in_specs=[pl.BlockSpec((tm, d), row), wspec(w1), wspec(w3), wspec(w2)],
            out_specs=pl.BlockSpec((tm, d), row),
            scratch_shapes=[pltpu.VMEM((tm, d), F32)]),
        out_shape=jax.ShapeDtypeStruct((n_slots, d), BF16),
        compiler_params=_cparams("arbitrary"),
        name="moe_ffn",
    )(tile_expert, tile_valid, hs, w1, w3, w2)


def _combine_kernel(x_ref, y1_ref, y2_ref, r_ref, gt_ref, o_ref):
    r = r_ref[...]
    y = r[:, 2:3] * y1_ref[...].astype(F32) + r[:, 3:4] * y2_ref[...].astype(F32)
    o_ref[...] = x_ref[...] + gt_ref[...] * y


def _combine(x, y1, y2, r, gt_tab, *, n_rows, seq, tm):
    n_all, d = x.shape
    n_mod = gt_tab.shape[0]

    def mod_idx(i):
        return (jnp.minimum(i * tm // seq, n_mod - 1), 0, 0)

    row = lambda i: (i, 0)
    return pl.pallas_call(
        _combine_kernel,
        grid=(n_rows // tm,),
        in_specs=[pl.BlockSpec((tm, d), row), pl.BlockSpec((tm, d), row), pl.BlockSpec((tm, d), row),
                  pl.BlockSpec((tm, LANES), row), pl.BlockSpec((None, 1, d), mod_idx)],
        out_specs=pl.BlockSpec((tm, d), row),
        out_shape=jax.ShapeDtypeStruct((n_all, d), F32),
        input_output_aliases={0: 0},
        compiler_params=_cparams("parallel"),
        name="moe_combine",
    )(x, y1, y2, r, gt_tab)


def _moe(x, a_tab, sh_tab, gt_tab, w_router, w1, w3, w2, *, n_rows, seq, tm):
    n_exp = w1.shape[0]
    h, r = _router(x, a_tab, sh_tab, w_router, n_rows=n_rows, seq=seq, tm=tm)
    e = r[:, :TOP_K].astype(jnp.int32).reshape(-1)
    onehot = (e[:, None] == jnp.arange(n_exp, dtype=jnp.int32)[None, :]).astype(jnp.int32)
    csum = jnp.cumsum(onehot, axis=0)
    rank = jnp.sum(csum * onehot, axis=1) - 1
    count = csum[-1]
    padded = (count + tm - 1) // tm * tm
    seg_end = jnp.cumsum(padded)
    seg_start = seg_end - padded
    slot = seg_start[e] + rank
    n_slots = TOP_K * n_rows + n_exp * tm
    pair_token = jnp.arange(TOP_K * n_rows, dtype=jnp.int32) // TOP_K
    slot_token = jnp.zeros((n_slots,), jnp.int32).at[slot].set(pair_token, unique_indices=True)
    tile_start = jnp.arange(n_slots // tm, dtype=jnp.int32) * tm
    tile_expert = jnp.minimum(jnp.searchsorted(seg_end, tile_start, side='right'), n_exp - 1).astype(jnp.int32)
    tile_valid = (tile_start < seg_end[-1]).astype(jnp.int32)
    hs = jnp.take(h, slot_token, axis=0)
    ys = _moe_ffn(tile_expert, tile_valid, hs, w1, w3, w2, tm=tm)
    slot2 = slot.reshape(n_rows, TOP_K)
    y1 = jnp.take(ys, slot2[:, 0], axis=0)
    y2 = jnp.take(ys, slot2[:, 1], axis=0)
    return _combine(x, y1, y2, r, gt_tab, n_rows=n_rows, seq=seq, tm=tm)


def _rope_tables(seq, tm):
    pairs = DIFF_HEAD_DIM // 4
    t = jnp.arange(seq, dtype=jnp.int32)
    row = (t // GRID_W).astype(F32)
    col = (t % GRID_W).astype(F32)
    inv = ROPE_BASE ** (-jnp.arange(pairs, dtype=F32) / pairs)
    ang = jnp.concatenate([row[:, None] * inv, col[:, None] * inv], axis=-1)
    cos = jnp.cos(ang)
    sin = jnp.sin(ang)
    reps = LANES // DIFF_HEAD_DIM
    cos_t = jnp.tile(jnp.concatenate([cos, cos], axis=-1), (1, reps))
    sin_t = jnp.tile(jnp.concatenate([-sin, sin], axis=-1), (1, reps))
    cos_t = jnp.concatenate([cos_t, jnp.ones((tm, LANES), F32)], axis=0)
    sin_t = jnp.concatenate([sin_t, jnp.zeros((tm, LANES), F32)], axis=0)
    return cos_t, sin_t


def _chunk_major(w, tf):
    *lead, d, f = w.shape
    w = w.reshape(*lead, d, f // tf, tf)
    return jnp.swapaxes(w, -3, -2)


def _ff_chunk(f):
    for tf in (256, 128):
        if f % tf == 0:
            return tf
    raise ValueError(f"hidden width {f} is not a multiple of {LANES}")


def kernel(x, c, ctx, c_ctx, w_mod, b_mod, g_mix, g_ffn, w_in, w_out, ssm_lam_re, ssm_lam_im, ssm_b_re, ssm_b_im, ssm_c_re, ssm_c_im, ssm_log_step, ssm_d, ssm_w_glu, ssm_b_glu, diff_lam_q1, diff_lam_k1, diff_lam_q2, diff_lam_k2, diff_subln_g, na_rpb, ffn_w1, ffn_w3, ffn_w2, moe_router, moe_w1, moe_w3, moe_w2, g_final):
    batch, seq, d = x.shape
    n_ctx = ctx.shape[1]
    depth = w_mod.shape[0]
    n_lat = batch * seq
    n_all = n_lat + batch * n_ctx
    ssm_w = d // 4
    diff_w = d // 2
    na_w = d - ssm_w - diff_w
    sizes = (ssm_w, diff_w, diff_w, diff_w, na_w, na_w, na_w)
    groups = ssm_w // SSM_GROUP
    tm = 512
    assert seq % tm == 0 and (batch * n_ctx) % tm == 0 and n_ctx % S5_CHUNK == 0 and seq % S5_CHUNK == 0
    assert seq % (GRID_W * NA_KEY_ROWS) == 0

    n_mod = batch + 1
    cond = jnp.concatenate([c, c_ctx[None, :]], axis=0).astype(F32)
    pad_rows = -n_mod % SUBLANES
    cond = jnp.pad(cond, ((0, pad_rows), (0, 0)))
    mods = _adaln(cond, w_mod.astype(F32), b_mod.astype(F32))[:, :n_mod]
    mods = mods.reshape(depth, n_mod, 6, 1, d)

    perm64 = np.concatenate([np.arange(0, DIFF_HEAD_DIM, 2), np.arange(1, DIFF_HEAD_DIM, 2)])
    col = np.arange(sum(sizes))
    qk = (col >= sizes[0]) & (col < sizes[0] + 2 * diff_w)
    base = (col - sizes[0]) // DIFF_HEAD_DIM * DIFF_HEAD_DIM + sizes[0]
    col_perm = np.where(qk, base + perm64[(col - sizes[0]) % DIFF_HEAD_DIM], col)
    w_in_b = w_in[:, :, col_perm].astype(BF16)
    w_out_b = w_out.astype(BF16)
    w_glu_b = ssm_w_glu.astype(BF16)
    tf = _ff_chunk(ffn_w1.shape[-1])
    ffn_w1_b = _chunk_major(ffn_w1.astype(BF16), tf)
    ffn_w3_b = _chunk_major(ffn_w3.astype(BF16), tf)
    ffn_w2_b = ffn_w2.astype(BF16).reshape(ffn_w2.shape[0], -1, tf, d)
    tfe = _ff_chunk(moe_w1.shape[-1])
    moe_w1_b = _chunk_major(moe_w1.astype(BF16), tfe)
    moe_w3_b = _chunk_major(moe_w3.astype(BF16), tfe)
    moe_w2_b = moe_w2.astype(BF16).reshape(moe_w2.shape[0], moe_w2.shape[1], -1, tfe, d)
    n_exp = moe_router.shape[-1]
    router_p = jnp.pad(moe_router.astype(F32), ((0, 0), (0, 0), (0, LANES - n_exp)))

    cos_tab, sin_tab = _rope_tables(seq, tm)
    n_chunks_ctx = n_ctx // S5_CHUNK
    n_chunks = (seq + n_ctx) // S5_CHUNK

    xs = jnp.concatenate([x.reshape(n_lat, d), ctx.reshape(batch * n_ctx, d)], axis=0).astype(F32)

    for layer in range(depth):
        need_ctx = layer < depth - 1
        n_rows = n_all if need_ctx else n_lat
        lam_init = 0.8 - 0.6 * math.exp(-0.3 * layer)
        sh1, sc1, gt1, sh2, sc2, gt2 = (mods[layer, :, i] for i in range(6))
        a1 = g_mix[layer].astype(F32) * (1.0 + sc1)
        a2 = g_ffn[layer].astype(F32) * (1.0 + sc2)

        u, dq, dk, dv, nq, nk, nv = _in_proj(xs, a1, sh1, w_in_b[layer], cos_tab, sin_tab,
                                             n_lat=n_lat, seq=seq, sizes=sizes, tm=tm)

        u_lat = u[:n_lat].reshape(batch, seq // S5_CHUNK, S5_CHUNK, groups, SSM_GROUP)
        u_ctx = u[n_lat:].reshape(batch, n_chunks_ctx, S5_CHUNK, groups, SSM_GROUP)
        ut = jnp.concatenate([u_ctx, u_lat], axis=1)
        ut = jnp.transpose(ut, (3, 1, 0, 2, 4)).reshape(groups, n_chunks, batch, S5_CHUNK * SSM_GROUP)
        s5m = _s5_matrices(ssm_lam_re[layer], ssm_lam_im[layer], ssm_b_re[layer], ssm_b_im[layer],
                           ssm_c_re[layer], ssm_c_im[layer], ssm_log_step[layer], ssm_d[layer])
        yt = _s5(ut, *s5m, n_ctx_chunks=n_chunks_ctx)
        yt = yt.reshape(groups, n_chunks, batch, S5_CHUNK, SSM_GROUP)
        yt = jnp.transpose(yt, (2, 1, 3, 0, 4)).reshape(batch, seq + n_ctx, ssm_w)
        y_ssm = jnp.concatenate([yt[:, n_ctx:].reshape(n_lat, ssm_w), yt[:, :n_ctx].reshape(-1, ssm_w)], axis=0)

        lam = (jnp.exp(jnp.sum(diff_lam_q1[layer].astype(F32) * diff_lam_k1[layer].astype(F32)))
               - jnp.exp(jnp.sum(diff_lam_q2[layer].astype(F32) * diff_lam_k2[layer].astype(F32))) + lam_init)
        lam = lam.reshape(1, 1)
        g_sub = (diff_subln_g[layer].astype(F32) * (1.0 - lam_init)).reshape(1, DIFF_V_DIM)
        diff = _diff_lat(lam, dq, dk, dv, g_sub, batch=batch, seq=seq, n_ctx=n_ctx, tq=256)
        bias = _na_bias(na_rpb[layer], seq // GRID_W)
        na = _na_lat(nq, nk, nv, bias, batch=batch, seq=seq, n_ctx=n_ctx)
        if need_ctx:
            diff = _diff_ctx(lam, dq, dk, dv, g_sub, diff, batch=batch, seq=seq, n_ctx=n_ctx)
            na = _na_ctx(nq, nk, nv, na, batch=batch, seq=seq, n_ctx=n_ctx)

        xs = _out_proj(xs, y_ssm, diff, na, w_glu_b[layer], ssm_b_glu[layer].astype(F32).reshape(1, -1),
                       w_out_b[layer], gt1, n_rows=n_rows, seq=seq, tm=tm)

        i = layer // 2
        if layer % 2 == 0:
            xs = _ffn(xs, a2, sh2, gt2, ffn_w1_b[i], ffn_w3_b[i], ffn_w2_b[i], n_rows=n_rows, seq=seq, tm=tm)
        else:
            xs = _moe(xs, a2, sh2, gt2, router_p[i], moe_w1_b[i], moe_w3_b[i], moe_w2_b[i],
                      n_rows=n_rows, seq=seq, tm=tm)

    return _final_norm(xs, g_final.astype(F32).reshape(1, d), n_rows=n_lat, tm=tm).reshape(batch, seq, d)


def _final_norm_kernel(x_ref, g_ref, o_ref):
    x = x_ref[...]
    ms = jnp.mean(x * x, axis=-1, keepdims=True)
    o_ref[...] = (x * lax.rsqrt(ms + NORM_EPS)) * g_ref[...]


def _final_norm(x, g, *, n_rows, tm):
    d = x.shape[1]
    return pl.pallas_call(
        _final_norm_kernel,
        grid=(n_rows // tm,),
        in_specs=[pl.BlockSpec((tm, d), lambda i: (i, 0)), pl.BlockSpec((1, d), lambda i: (0, 0))],
        out_specs=pl.BlockSpec((tm, d), lambda i: (i, 0)),
        out_shape=jax.ShapeDtypeStruct((n_rows, d), F32),
        compiler_params=_cparams("parallel"),
        name="final_norm",
    )(x, g)
```

```python
import functools
import math

import numpy as np
import jax
import jax.numpy as jnp
from jax import lax
from jax.experimental import pallas as pl
from jax.experimental.pallas import tpu as pltpu

F32 = jnp.float32
BF16 = jnp.bfloat16

GRID_W = 64
SSM_GROUP = 16
SSM_STATE = 64
DIFF_HEAD_DIM = 64
DIFF_V_DIM = 2 * DIFF_HEAD_DIM
NA_HEAD_DIM = 64
NA_WIN_ROWS = 8
NA_WIN_COLS = 16
ROPE_BASE = 10000.0
TOP_K = 2
NORM_EPS = 1e-6
SUBLN_EPS = 1e-5
NEG_INF = -1e30

LANES = 128
SUBLANES = 8
VMEM_LIMIT_BYTES = 56 * 1024 * 1024

MXU_DIM = 256
FF_CHUNK = MXU_DIM
LOG2E = math.log2(math.e)
S5_CHUNK = 16
NA_BAND_ROWS = 8
NA_KEY_ROWS = 2 * NA_WIN_ROWS

_DN_T = (((1,), (1,)), ((), ()))


def _cparams(*sem):
    return pltpu.CompilerParams(dimension_semantics=sem, vmem_limit_bytes=VMEM_LIMIT_BYTES)


def _dot(a, b):
    return jnp.dot(a, b, preferred_element_type=F32)


def _dot_t(a, b):
    return lax.dot_general(a, b, _DN_T, preferred_element_type=F32)


def _split_bf16(a):
    hi = a.astype(BF16)
    lo = (a - hi.astype(F32)).astype(BF16)
    return hi, lo


def _dot3(a, b):
    a_hi, a_lo = _split_bf16(a)
    b_hi, b_lo = _split_bf16(b)
    return _dot(a_hi, b_hi) + _dot(a_hi, b_lo) + _dot(a_lo, b_hi)


def _norm_mod(x, a, sh):
    ms = jnp.mean(x * x, axis=-1, keepdims=True)
    return (x * lax.rsqrt(ms + NORM_EPS)) * a + sh


def _adaln_kernel(c_ref, w_ref, b_ref, o_ref):
    c = c_ref[...]
    s = c * jax.nn.sigmoid(c)
    o_ref[...] = _dot3(s, w_ref[...]) + b_ref[...]


def _adaln(cond, w_mod, b_mod):
    depth, d, n6 = w_mod.shape
    r = cond.shape[0]
    tn = n6 // 6
    return pl.pallas_call(
        _adaln_kernel,
        grid=(depth, n6 // tn),
        in_specs=[pl.BlockSpec((r, d), lambda l, j: (0, 0)),
                  pl.BlockSpec((None, d, tn), lambda l, j: (l, 0, j)),
                  pl.BlockSpec((None, 1, tn), lambda l, j: (l, 0, j))],
        out_specs=pl.BlockSpec((None, r, tn), lambda l, j: (l, 0, j)),
        out_shape=jax.ShapeDtypeStruct((depth, r, n6), F32),
        compiler_params=_cparams("arbitrary", "arbitrary"),
        name="adaln",
    )(cond, w_mod, b_mod.reshape(depth, 1, n6))


def _in_proj_kernel(x_ref, a_ref, sh_ref, w_ref, cos_ref, sin_ref,
                    u_ref, q_ref, k_ref, v_ref, nq_ref, nk_ref, nv_ref, *, sizes):
    h = _norm_mod(x_ref[...], a_ref[...], sh_ref[...]).astype(BF16)
    p = _dot(h, w_ref[...])
    cs = cos_ref[...]
    sn = sin_ref[...]
    lane = lax.broadcasted_iota(jnp.int32, cs.shape, 1)
    first = (lane % DIFF_HEAD_DIM) < (DIFF_HEAD_DIM // 2)

    def rope(t):
        partner = jnp.where(first, pltpu.roll(t, LANES - DIFF_HEAD_DIM // 2, 1),
                            pltpu.roll(t, DIFF_HEAD_DIM // 2, 1))
        return t * cs + partner * sn

    o = [sum(sizes[:i]) for i in range(len(sizes) + 1)]
    u_ref[...] = p[:, o[0]:o[1]].astype(BF16)
    qk_scale = DIFF_HEAD_DIM ** -0.5 * LOG2E
    for c0 in range(0, sizes[1], LANES):
        q_ref[:, c0:c0 + LANES] = (rope(p[:, o[1] + c0:o[1] + c0 + LANES]) * qk_scale).astype(BF16)
        k_ref[:, c0:c0 + LANES] = rope(p[:, o[2] + c0:o[2] + c0 + LANES]).astype(BF16)
    v_ref[...] = p[:, o[3]:o[4]].astype(BF16)
    nq_ref[...] = (p[:, o[4]:o[5]] * (NA_HEAD_DIM ** -0.5 * LOG2E)).astype(BF16)
    nk_ref[...] = p[:, o[5]:o[6]].astype(BF16)
    nv_ref[...] = p[:, o[6]:o[7]].astype(BF16)


def _in_proj(x, a_tab, sh_tab, w_in, cos_tab, sin_tab, *, n_lat, seq, sizes, tm):
    n_all, d = x.shape
    n_tiles = n_all // tm
    n_lat_tiles = n_lat // tm
    seq_tiles = seq // tm
    n_mod = a_tab.shape[0]

    def mod_idx(i):
        return (jnp.minimum(i * tm // seq, n_mod - 1), 0, 0)

    def rope_idx(i):
        return (jnp.where(i < n_lat_tiles, i % seq_tiles, seq_tiles), 0)

    row = lambda i: (i, 0)
    return pl.pallas_call(
        functools.partial(_in_proj_kernel, sizes=sizes),
        grid=(n_tiles,),
        in_specs=[pl.BlockSpec((tm, d), row),
                  pl.BlockSpec((None, 1, d), mod_idx),
                  pl.BlockSpec((None, 1, d), mod_idx),
                  pl.BlockSpec(w_in.shape, lambda i: (0, 0)),
                  pl.BlockSpec((tm, LANES), rope_idx),
                  pl.BlockSpec((tm, LANES), rope_idx)],
        out_specs=[pl.BlockSpec((tm, s), row) for s in sizes],
        out_shape=[jax.ShapeDtypeStruct((n_all, s), BF16) for s in sizes],
        compiler_params=_cparams("parallel"),
        name="in_proj",
    )(x, a_tab, sh_tab, w_in, cos_tab, sin_tab)


def _s5_kernel(u_ref, m_ref, e_ref, f_ref, a_ref, d_ref, y_ref, v_scr, s_scr, *, n_ctx_chunks):
    n_chunks, nb, width = u_ref.shape
    rows = n_chunks * nb
    half = width // 2
    u = u_ref[...].reshape(rows, width)
    y = u.astype(F32) * d_ref[...]
    for direction in range(2):
        y = y + _dot(u, m_ref[direction])
        v_scr[...] = _dot(u, e_ref[direction])
        a1 = jnp.broadcast_to(a_ref[direction, 0:1, :], (nb, half))
        a2 = jnp.broadcast_to(a_ref[direction, 1:2, :], (nb, half))
        a2s = jnp.broadcast_to(a_ref[direction, 2:3, :], (nb, half))

        def step(i, carry, direction=direction, a1=a1, a2=a2, a2s=a2s):
            s, ssw = carry
            if direction == 0:
                c = i
            else:
                c = jnp.where(i < n_ctx_chunks, n_ctx_chunks - 1 - i, n_chunks - 1 + n_ctx_chunks - i)
            r0 = pl.multiple_of(c * nb, nb)
            s_scr[pl.ds(r0, nb), :] = s
            v = v_scr[pl.ds(r0, nb), :]
            return (a1 * s + a2 * ssw + v[:, :half], a1 * ssw + a2s * s + v[:, half:])

        zero = jnp.zeros((nb, half), F32)
        lax.fori_loop(0, n_chunks, step, (zero, zero))
        y = y + _dot(s_scr[...].astype(BF16), f_ref[direction])
    y_ref[...] = y.reshape(n_chunks, nb, width).astype(BF16)


def _s5(ut, m, e, f, a, dsk, *, n_ctx_chunks):
    g, n_chunks, nb, width = ut.shape
    rows = n_chunks * nb
    return pl.pallas_call(
        functools.partial(_s5_kernel, n_ctx_chunks=n_ctx_chunks),
        grid=(g,),
        in_specs=[pl.BlockSpec((None, n_chunks, nb, width), lambda i: (i, 0, 0, 0)),
                  pl.BlockSpec((None, 2, width, width), lambda i: (i, 0, 0, 0)),
                  pl.BlockSpec((None, 2, width, width), lambda i: (i, 0, 0, 0)),
                  pl.BlockSpec((None, 2, width // 2, width), lambda i: (i, 0, 0, 0)),
                  pl.BlockSpec((None, 2, SUBLANES, width // 2), lambda i: (i, 0, 0, 0)),
                  pl.BlockSpec((None, 1, width), lambda i: (i, 0, 0))],
        out_specs=pl.BlockSpec((None, n_chunks, nb, width), lambda i: (i, 0, 0, 0)),
        out_shape=jax.ShapeDtypeStruct(ut.shape, BF16),
        scratch_shapes=[pltpu.VMEM((rows, width), F32), pltpu.VMEM((rows, width // 2), F32)],
        compiler_params=_cparams("parallel"),
        name="s5",
    )(ut, m, e, f, a, dsk)


def _s5_matrices(lam_re, lam_im, b_re, b_im, c_re, c_im, log_step, d_skip):
    t = S5_CHUNK
    dt = jnp.exp(log_step.astype(F32))[..., None]
    lr = lam_re.astype(F32)
    li = lam_im.astype(F32)
    mag = jnp.exp(lr * dt)
    ab_re = mag * jnp.cos(li * dt)
    ab_im = mag * jnp.sin(li * dt)
    den = lr * lr + li * li
    n_re = ab_re - 1.0
    n_im = ab_im
    f_re = (n_re * lr + n_im * li) / den
    f_im = (n_im * lr - n_re * li) / den
    br = b_re.astype(F32)
    bi = b_im.astype(F32)
    bb_re = f_re[..., None] * br - f_im[..., None] * bi
    bb_im = f_re[..., None] * bi + f_im[..., None] * br
    j = jnp.arange(t + 1, dtype=F32)[:, None, None, None]
    pw_mag = jnp.exp(lr * dt * j)
    pw_re = pw_mag * jnp.cos(li * dt * j)
    pw_im = pw_mag * jnp.sin(li * dt * j)
    cr = c_re.astype(F32)
    ci = c_im.astype(F32)
    cp_re = cr[None] * pw_re[:, :, :, None, :] - ci[None] * pw_im[:, :, :, None, :]
    cp_im = cr[None] * pw_im[:, :, :, None, :] + ci[None] * pw_re[:, :, :, None, :]
    taps = (jnp.einsum('jdghp,dgpk->jdghk', cp_re[:t], bb_re)
            - jnp.einsum('jdghp,dgpk->jdghk', cp_im[:t], bb_im))
    tau = np.arange(t)[:, None]
    tt = np.arange(t)[None, :]
    mats = []
    for direction, lag in enumerate((tt - tau, tau - tt)):
        valid = lag >= 0
        kk = taps[np.clip(lag, 0, t - 1), direction]
        kk = jnp.where(valid[:, :, None, None, None], kk, 0.0)
        mats.append(jnp.transpose(kk, (2, 0, 4, 1, 3)))
    m = jnp.stack(mats, axis=1)
    gcount = m.shape[0]
    hdim = SSM_GROUP
    m = m.reshape(gcount, 2, t * hdim, t * hdim)
    es = []
    for direction, power in enumerate((t - 1 - np.arange(t), np.arange(t))):
        pr = pw_re[power, direction]
        pi = pw_im[power, direction]
        e_re = pr[:, :, :, None] * bb_re[direction][None] - pi[:, :, :, None] * bb_im[direction][None]
        e_im = pr[:, :, :, None] * bb_im[direction][None] + pi[:, :, :, None] * bb_re[direction][None]
        e_re = jnp.transpose(e_re, (1, 0, 3, 2)).reshape(gcount, t * hdim, -1)
        e_im = jnp.transpose(e_im, (1, 0, 3, 2)).reshape(gcount, t * hdim, -1)
        es.append(jnp.concatenate([e_re, e_im, e_im, e_re], axis=-1))
    e = jnp.stack(es, axis=1)
    fs = []
    for direction, power in enumerate((np.arange(t) + 1, t - np.arange(t))):
        fr = cp_re[power, direction]
        fi = cp_im[power, direction]
        fr = jnp.transpose(fr, (1, 3, 0, 2)).reshape(gcount, -1, t * hdim)
        fi = jnp.transpose(fi, (1, 3, 0, 2)).reshape(gcount, -1, t * hdim)
        fs.append(jnp.concatenate([fr, -fi], axis=1))
    f = jnp.stack(fs, axis=1)
    ar = jnp.transpose(pw_re[t], (1, 0, 2))
    ai = jnp.transpose(pw_im[t], (1, 0, 2))
    rows = jnp.stack([jnp.concatenate([ar, ar], -1), jnp.concatenate([-ai, ai], -1),
                      jnp.concatenate([ai, -ai], -1)], axis=2)
    a = jnp.pad(rows, ((0, 0), (0, 0), (0, SUBLANES - 3), (0, 0)))
    dsk = jnp.tile(d_skip.astype(F32).reshape(gcount, 1, hdim), (1, 1, t))
    return m.astype(BF16), e.astype(BF16), f.astype(BF16), a, dsk


def _diff_fill(kl_ref, kc_ref, vl_ref, vc_ref, kp_scr, vt_scr):
    hd = DIFF_V_DIM
    n_keys = kp_scr.shape[0]
    n_lat = n_keys - kc_ref.shape[0]
    if kl_ref is not None:
        kp_scr[0:n_lat, 0:hd] = kl_ref[...]
        vt_scr[0:hd, 0:n_lat] = vl_ref[...].astype(F32).T.astype(BF16)
    kp_scr[n_lat:, 0:hd] = kc_ref[...]
    kp_scr[:, hd:] = jnp.zeros((n_keys, MXU_DIM - hd), BF16)
    vt_scr[0:hd, n_lat:] = vc_ref[...].astype(F32).T.astype(BF16)
    vt_scr[hd:, :] = jnp.ones((vt_scr.shape[0] - hd, n_keys), BF16)


def _diff_scores(q_ref, kp_scr, chunks):
    hd = DIFF_V_DIM
    tq = q_ref.shape[0]
    qt = q_ref[...].astype(F32).T
    row = lax.broadcasted_iota(jnp.int32, qt.shape, 0)
    q2t = jnp.concatenate([jnp.where(row < DIFF_HEAD_DIM, qt, 0.0), jnp.where(row >= DIFF_HEAD_DIM, qt, 0.0)], axis=1)
    q2t = jnp.concatenate([q2t, jnp.zeros((MXU_DIM - hd, 2 * tq), F32)], axis=0).astype(BF16)
    scores = []
    mx = None
    for c0, kn in chunks:
        s = _dot(kp_scr[c0:c0 + kn, :], q2t)
        scores.append(s)
        part = jnp.max(s.reshape(kn // SUBLANES, SUBLANES, 2 * tq), axis=0)
        mx = part if mx is None else jnp.maximum(mx, part)
    return scores, jnp.max(mx, axis=0, keepdims=True)


def _diff_output(scores, m, vt_scr, chunks, lam, g_ref, o_ref):
    hd = DIFF_V_DIM
    tq = o_ref.shape[0]
    acc = None
    for (c0, kn), s in zip(chunks, scores):
        p = jnp.exp2((s - m).astype(BF16))
        d = _dot(vt_scr[:, c0:c0 + kn], p)
        acc = d if acc is None else acc + d
    r = acc[0:hd, :] / acc[hd:hd + 1, :]
    ot = r[:, :tq] - lam * r[:, tq:]
    ms = jnp.mean(ot * ot, axis=0, keepdims=True)
    ot = ot * lax.rsqrt(ms + SUBLN_EPS) * g_ref[...]
    o_ref[...] = ot.T.astype(BF16)


def _diff_lat_kernel(lam_ref, q_ref, qn_ref, kl_ref, kc_ref, vl_ref, vc_ref, g_ref, o_ref,
                     kp_scr, vt_scr, m_a, m_b, *s_scrs, key_chunk):
    j = pl.program_id(2)
    tq = qn_ref.shape[0]
    n_keys = kp_scr.shape[0]
    chunks = _diff_chunks(n_keys - kc_ref.shape[0], kc_ref.shape[0], key_chunk)
    s_a, s_b = s_scrs[:len(chunks)], s_scrs[len(chunks):]
    lam = lam_ref[0, 0]

    def store_scores(src_ref, s_dst, m_dst):
        scores, m = _diff_scores(src_ref, kp_scr, chunks)
        for s, s_scr in zip(scores, s_dst):
            s_scr[...] = s
        m_dst[...] = m

    def finish(s_src, m_src, dst_ref):
        _diff_output([s_scr[...] for s_scr in s_src], m_src[...], vt_scr, chunks, lam, g_ref, dst_ref)

    @pl.when(j == 0)
    def _():
        _diff_fill(kl_ref, kc_ref, vl_ref, vc_ref, kp_scr, vt_scr)
        store_scores(q_ref.at[0:tq], s_a, m_a)

    store_scores(q_ref.at[tq:2 * tq], s_b, m_b)
    finish(s_a, m_a, o_ref.at[0:tq])
    store_scores(qn_ref, s_a, m_a)
    finish(s_b, m_b, o_ref.at[tq:2 * tq])


def _diff_chunks(n_lat, n_ctx, key_chunk):
    chunks = [(k0, min(key_chunk, n_lat - k0)) for k0 in range(0, n_lat, key_chunk)] if n_lat else []
    return chunks + [(n_lat, n_ctx)]


def _diff_scratch(n_lat, n_ctx):
    ones_rows = 2 * SUBLANES
    return [pltpu.VMEM((n_lat + n_ctx, MXU_DIM), BF16), pltpu.VMEM((DIFF_V_DIM + ones_rows, n_lat + n_ctx), BF16)]


def _diff_lat(lam, q, k, v, g, *, batch, seq, n_ctx, tq, key_chunk):
    n_all, width = q.shape
    heads = width // DIFF_V_DIM
    q_tiles = seq // tq
    ctx_blk0 = batch * seq // n_ctx
    smem = pl.BlockSpec(memory_space=pltpu.SMEM)
    steps = q_tiles // 2
    qspec = pl.BlockSpec((2 * tq, DIFF_V_DIM), lambda b, h, i: (b * steps + i, h))
    qnext = pl.BlockSpec((tq, DIFF_V_DIM), lambda b, h, i: (b * q_tiles + jnp.minimum(2 * i + 2, q_tiles - 1), h))
    lat = pl.BlockSpec((seq, DIFF_V_DIM), lambda b, h, i: (b, h))
    ctx = pl.BlockSpec((n_ctx, DIFF_V_DIM), lambda b, h, i: (ctx_blk0 + b, h))
    chunk_scores = [pltpu.VMEM((kn, 2 * tq), F32) for _, kn in _diff_chunks(seq, n_ctx, key_chunk)]
    pipeline_scratch = [pltpu.VMEM((1, 2 * tq), F32)] * 2 + chunk_scores * 2
    return pl.pallas_call(
        functools.partial(_diff_lat_kernel, key_chunk=key_chunk),
        grid=(batch, heads, steps),
        in_specs=[smem, qspec, qnext, lat, ctx, lat, ctx, pl.BlockSpec((DIFF_V_DIM, 1), lambda b, h, i: (0, 0))],
        out_specs=qspec,
        out_shape=jax.ShapeDtypeStruct((n_all, width), BF16),
        scratch_shapes=_diff_scratch(seq, n_ctx) + pipeline_scratch,
        compiler_params=_cparams("arbitrary", "arbitrary", "arbitrary"),
        name="diff_lat",
    )(lam, q, q, k, k, v, v, g)


def _diff_ctx_kernel(lam_ref, q_ref, kc_ref, vc_ref, g_ref, prev_ref, o_ref, kp_scr, vt_scr):
    del prev_ref
    chunks = _diff_chunks(0, kc_ref.shape[0], 0)
    _diff_fill(None, kc_ref, None, vc_ref, kp_scr, vt_scr)
    scores, m = _diff_scores(q_ref, kp_scr, chunks)
    _diff_output(scores, m, vt_scr, chunks, lam_ref[0, 0], g_ref, o_ref)


def _diff_ctx(lam, q, k, v, g, out, *, batch, seq, n_ctx):
    n_all, width = q.shape
    heads = width // DIFF_V_DIM
    ctx_blk0 = batch * seq // n_ctx
    smem = pl.BlockSpec(memory_space=pltpu.SMEM)
    ctx = pl.BlockSpec((n_ctx, DIFF_V_DIM), lambda b, h: (ctx_blk0 + b, h))
    return pl.pallas_call(
        _diff_ctx_kernel,
        grid=(batch, heads),
        in_specs=[smem, ctx, ctx, ctx, pl.BlockSpec((DIFF_V_DIM, 1), lambda b, h: (0, 0)),
                  pl.BlockSpec(memory_space=pl.ANY)],
        out_specs=ctx,
        out_shape=jax.ShapeDtypeStruct((n_all, width), BF16),
        scratch_shapes=_diff_scratch(0, n_ctx),
        input_output_aliases={5: 0},
        compiler_params=_cparams("parallel", "parallel"),
        name="diff_ctx",
    )(lam, q, k, v, g, out)


def _na_kernel(q_ref, kl_ref, vl_ref, kc_ref, vc_ref, bias_ref, o_ref, *, grid_rows):
    band = pl.program_id(0)
    key_tokens = NA_KEY_ROWS * GRID_W
    kstart = jnp.clip(band * NA_BAND_ROWS - (NA_KEY_ROWS - NA_BAND_ROWS) // 2, 0, grid_rows - NA_KEY_ROWS)
    off = pl.multiple_of(kstart * GRID_W, (NA_KEY_ROWS - NA_BAND_ROWS) // 2 * GRID_W)
    kw = kl_ref[pl.ds(off, key_tokens), :]
    vw = vl_ref[pl.ds(off, key_tokens), :]
    q = q_ref[...]
    lane = lax.broadcasted_iota(jnp.int32, q.shape, 1)
    acc = jnp.zeros(q.shape, F32)
    for h in range(q.shape[1] // NA_HEAD_DIM):
        mine = (lane >= h * NA_HEAD_DIM) & (lane < (h + 1) * NA_HEAD_DIM)
        qh = jnp.where(mine, q, jnp.zeros_like(q))
        sl = _dot_t(qh, kw) + bias_ref[h]
        sc = _dot_t(qh, kc_ref[...])
        m = jnp.maximum(jnp.max(sl, axis=-1, keepdims=True), jnp.max(sc, axis=-1, keepdims=True))
        pl_ = jnp.exp2(sl - m)
        pc = jnp.exp2(sc - m)
        l = jnp.sum(pl_, axis=-1, keepdims=True) + jnp.sum(pc, axis=-1, keepdims=True)
        o = (_dot(pl_.astype(BF16), vw) + _dot(pc.astype(BF16), vc_ref[...])) / l
        acc = jnp.where(mine, o, acc)
    o_ref[...] = acc.astype(BF16)


def _na_lat(q, k, v, bias, *, batch, seq, n_ctx):
    n_all, width = q.shape
    grid_rows = seq // GRID_W
    n_bands = grid_rows // NA_BAND_ROWS
    tq = NA_BAND_ROWS * GRID_W
    ctx_blk0 = batch * seq // n_ctx
    heads = width // NA_HEAD_DIM

    def variant(kb, b):
        return (jnp.where(kb == 0, 0, jnp.where(kb == n_bands - 1, 2, 1)), 0, 0, 0)

    qspec = pl.BlockSpec((tq, width), lambda kb, b: (b * n_bands + kb, 0))
    lat = pl.BlockSpec((seq, width), lambda kb, b: (b, 0))
    ctx = pl.BlockSpec((n_ctx, width), lambda kb, b: (ctx_blk0 + b, 0))
    return pl.pallas_call(
        functools.partial(_na_kernel, grid_rows=grid_rows),
        grid=(n_bands, batch),
        in_specs=[qspec, lat, lat, ctx, ctx,
                  pl.BlockSpec((None, heads, tq, NA_KEY_ROWS * GRID_W), variant)],
        out_specs=qspec,
        out_shape=jax.ShapeDtypeStruct((n_all, width), BF16),
        compiler_params=_cparams("arbitrary", "arbitrary"),
        name="na_lat",
    )(q, k, v, k, v, bias)


def _na_bias(rpb, grid_rows):
    n_bands = grid_rows // NA_BAND_ROWS
    qr = np.arange(NA_BAND_ROWS)[:, None]
    kr = np.arange(NA_KEY_ROWS)[None, :]
    qc = np.arange(GRID_W)[:, None]
    kc = np.arange(GRID_W)[None, :]
    cstart = np.clip(qc - NA_WIN_COLS // 2, 0, GRID_W - NA_WIN_COLS)
    col_valid = (kc >= cstart) & (kc < cstart + NA_WIN_COLS)
    col_idx = np.clip(kc - qc + NA_WIN_COLS - 1, 0, 2 * NA_WIN_COLS - 2)
    n_col = 2 * NA_WIN_COLS - 1
    col_onehot = (col_idx[None] == np.arange(n_col)[:, None, None]).astype(np.float32)
    rpb = rpb.astype(F32)
    tables = []
    for band in (0, min(1, n_bands - 1), n_bands - 1):
        r = band * NA_BAND_ROWS + qr
        start = np.clip(r - NA_WIN_ROWS // 2, 0, grid_rows - NA_WIN_ROWS)
        kstart = np.clip(band * NA_BAND_ROWS - (NA_KEY_ROWS - NA_BAND_ROWS) // 2, 0, grid_rows - NA_KEY_ROWS)
        krow = kstart + kr
        row_valid = (krow >= start) & (krow < start + NA_WIN_ROWS)
        row_idx = np.clip(krow - r + NA_WIN_ROWS - 1, 0, 2 * NA_WIN_ROWS - 2)
        by_row = rpb[:, row_idx]
        tab = jnp.einsum('hqkc,cxy->hqxky', by_row, col_onehot, precision=lax.Precision.HIGHEST)
        valid = row_valid[:, None, :, None] & col_valid[None, :, None, :]
        tab = jnp.where(valid[None], tab * LOG2E, NEG_INF)
        tables.append(tab.reshape(tab.shape[0], NA_BAND_ROWS * GRID_W, NA_KEY_ROWS * GRID_W))
    return jnp.stack(tables, axis=0)


def _na_ctx_kernel(q_ref, k_ref, v_ref, prev_ref, o_ref):
    del prev_ref
    q = q_ref[...]
    lane = lax.broadcasted_iota(jnp.int32, q.shape, 1)
    acc = jnp.zeros(q.shape, F32)
    for h in range(q.shape[1] // NA_HEAD_DIM):
        mine = (lane >= h * NA_HEAD_DIM) & (lane < (h + 1) * NA_HEAD_DIM)
        qh = jnp.where(mine, q, jnp.zeros_like(q))
        s = _dot_t(qh, k_ref[...])
        p = jnp.exp2(s - jnp.max(s, axis=-1, keepdims=True))
        o = _dot(p.astype(BF16), v_ref[...]) / jnp.sum(p, axis=-1, keepdims=True)
        acc = jnp.where(mine, o, acc)
    o_ref[...] = acc.astype(BF16)


def _na_ctx(q, k, v, out, *, batch, seq, n_ctx):
    n_all, width = q.shape
    ctx_blk0 = batch * seq // n_ctx
    ctx = pl.BlockSpec((n_ctx, width), lambda b: (ctx_blk0 + b, 0))
    return pl.pallas_call(
        _na_ctx_kernel,
        grid=(batch,),
        in_specs=[ctx, ctx, ctx, pl.BlockSpec(memory_space=pl.ANY)],
        out_specs=ctx,
        out_shape=jax.ShapeDtypeStruct((n_all, width), BF16),
        input_output_aliases={3: 0},
        compiler_params=_cparams("parallel"),
        name="na_ctx",
    )(q, k, v, out)


def _gelu_tanh(x):
    return 0.5 * x * (1.0 + jnp.tanh(math.sqrt(2.0 / math.pi) * (x + 0.044715 * (x * x * x))))


def _out_proj_kernel(x_ref, y_ref, df_ref, na_ref, wg_ref, bg_ref, wo_ref, gt_ref, o_ref):
    g = _gelu_tanh(y_ref[...].astype(F32))
    ssm = g * jax.nn.sigmoid(_dot(g.astype(BF16), wg_ref[...]) + bg_ref[...])
    w_ssm = y_ref.shape[1]
    w_diff = df_ref.shape[1]
    acc = _dot(ssm.astype(BF16), wo_ref[0:w_ssm, :])
    acc = acc + _dot(df_ref[...], wo_ref[w_ssm:w_ssm + w_diff, :])
    acc = acc + _dot(na_ref[...], wo_ref[w_ssm + w_diff:, :])
    o_ref[...] = x_ref[...] + gt_ref[...] * acc


def _out_proj(x, y_ssm, diff, na, w_glu, b_glu, w_out, gt_tab, *, n_rows, seq, tm):
    n_all, d = x.shape
    n_mod = gt_tab.shape[0]

    def mod_idx(i):
        return (jnp.minimum(i * tm // seq, n_mod - 1), 0, 0)

    row = lambda i: (i, 0)
    full = lambda i: (0, 0)
    return pl.pallas_call(
        _out_proj_kernel,
        grid=(n_rows // tm,),
        in_specs=[pl.BlockSpec((tm, d), row),
                  pl.BlockSpec((tm, y_ssm.shape[1]), row),
                  pl.BlockSpec((tm, diff.shape[1]), row),
                  pl.BlockSpec((tm, na.shape[1]), row),
                  pl.BlockSpec(w_glu.shape, full),
                  pl.BlockSpec(b_glu.shape, full),
                  pl.BlockSpec(w_out.shape, full),
                  pl.BlockSpec((None, 1, d), mod_idx)],
        out_specs=pl.BlockSpec((tm, d), row),
        out_shape=jax.ShapeDtypeStruct((n_all, d), F32),
        input_output_aliases={0: 0},
        compiler_params=_cparams("parallel"),
        name="out_proj",
    )(x, y_ssm, diff, na, w_glu, b_glu, w_out, gt_tab)


def _swiglu(h, w1_ref, w3_ref, w2_ref, act_ref):
    f_total = w1_ref.shape[1]
    tf = FF_CHUNK if f_total % FF_CHUNK == 0 else LANES
    for f0 in range(0, f_total, tf):
        a = _dot(h, w1_ref[:, f0:f0 + tf])
        b = _dot(h, w3_ref[:, f0:f0 + tf])
        act_ref[:, f0:f0 + tf] = (a * jax.nn.sigmoid(a) * b).astype(BF16)
    return _dot(act_ref[...], w2_ref[...])


def _ffn_kernel(x_ref, a_ref, sh_ref, gt_ref, w1_ref, w3_ref, w2_ref, o_ref, act_ref):
    x = x_ref[...]
    h = _norm_mod(x, a_ref[...], sh_ref[...]).astype(BF16)
    o_ref[...] = x + gt_ref[...] * _swiglu(h, w1_ref, w3_ref, w2_ref, act_ref)


def _resident(shape, index_map):
    return pl.BlockSpec(shape, index_map, pipeline_mode=pl.Buffered(1))


def _ffn(x, a_tab, sh_tab, gt_tab, w1, w3, w2, *, n_rows, seq, tm):
    n_all, d = x.shape
    n_mod = a_tab.shape[0]

    def mod_idx(i):
        return (jnp.minimum(i * tm // seq, n_mod - 1), 0, 0)

    row = lambda i: (i, 0)
    full = lambda i: (0, 0)
    mod = pl.BlockSpec((None, 1, d), mod_idx)
    return pl.pallas_call(
        _ffn_kernel,
        grid=(n_rows // tm,),
        in_specs=[pl.BlockSpec((tm, d), row), mod, mod, mod,
                  _resident(w1.shape, full), _resident(w3.shape, full), _resident(w2.shape, full)],
        out_specs=pl.BlockSpec((tm, d), row),
        out_shape=jax.ShapeDtypeStruct((n_all, d), F32),
        scratch_shapes=[pltpu.VMEM((tm, w1.shape[1]), BF16)],
        input_output_aliases={0: 0},
        compiler_params=_cparams("parallel"),
        name="ffn",
    )(x, a_tab, sh_tab, gt_tab, w1, w3, w2)


def _router_kernel(x_ref, a_ref, sh_ref, wr_ref, h_ref, r_ref):
    h = _norm_mod(x_ref[...], a_ref[...], sh_ref[...])
    h_ref[...] = h.astype(BF16)
    n_exp = 8
    logits = _dot3(h, wr_ref[...])
    lane = lax.broadcasted_iota(jnp.int32, logits.shape, 1)
    logits = jnp.where(lane < n_exp, logits, -jnp.inf)
    m1 = jnp.max(logits, axis=-1, keepdims=True)
    i1 = jnp.min(jnp.where(logits == m1, lane, LANES), axis=-1, keepdims=True)
    rest = jnp.where(lane == i1, -jnp.inf, logits)
    m2 = jnp.max(rest, axis=-1, keepdims=True)
    i2 = jnp.min(jnp.where(rest == m2, lane, LANES), axis=-1, keepdims=True)
    e2 = jnp.exp(m2 - m1)
    g1 = 1.0 / (1.0 + e2)
    g2 = e2 / (1.0 + e2)
    r_ref[...] = jnp.where(lane == 0, i1.astype(F32),
                           jnp.where(lane == 1, i2.astype(F32),
                                     jnp.where(lane == 2, g1, jnp.where(lane == 3, g2, 0.0))))


def _router(x, a_tab, sh_tab, w_router, *, n_rows, seq, tm):
    n_all, d = x.shape
    n_mod = a_tab.shape[0]

    def mod_idx(i):
        return (jnp.minimum(i * tm // seq, n_mod - 1), 0, 0)

    row = lambda i: (i, 0)
    mod = pl.BlockSpec((None, 1, d), mod_idx)
    return pl.pallas_call(
        _router_kernel,
        grid=(n_rows // tm,),
        in_specs=[pl.BlockSpec((tm, d), row), mod, mod, pl.BlockSpec(w_router.shape, lambda i: (0, 0))],
        out_specs=[pl.BlockSpec((tm, d), row), pl.BlockSpec((tm, LANES), row)],
        out_shape=[jax.ShapeDtypeStruct((n_rows, d), BF16), jax.ShapeDtypeStruct((n_rows, LANES), F32)],
        compiler_params=_cparams("parallel"),
        name="router",
    )(x, a_tab, sh_tab, w_router)


def _moe_ffn_kernel(te_ref, tv_ref, h_ref, w1_ref, w3_ref, w2_ref, o_ref, act_ref):
    i = pl.program_id(0)

    @pl.when(tv_ref[i] > 0)
    def _():
        o_ref[...] = _swiglu(h_ref[...], w1_ref, w3_ref, w2_ref, act_ref).astype(BF16)

    @pl.when(tv_ref[i] == 0)
    def _():
        o_ref[...] = jnp.zeros(o_ref.shape, BF16)


def _moe_ffn(tile_expert, tile_valid, hs, w1, w3, w2, *, tm):
    n_slots, d = hs.shape
    wspec = lambda w: _resident((None,) + w.shape[1:], lambda i, te, tv: (te[i], 0, 0))
    row = lambda i, te, tv: (i, 0)
    return pl.pallas_call(
        _moe_ffn_kernel,
        grid_spec=pltpu.PrefetchScalarGridSpec(
            num_scalar_prefetch=2,
            grid=(n_slots // tm,),
            in_specs=[pl.BlockSpec((tm, d), row), wspec(w1), wspec(w3), wspec(w2)],
            out_specs=pl.BlockSpec((tm, d), row),
            scratch_shapes=[pltpu.VMEM((tm, w1.shape[2]), BF16)]),
        out_shape=jax.ShapeDtypeStruct((n_slots, d), BF16),
        compiler_params=_cparams("arbitrary"),
        name="moe_ffn",
    )(tile_expert, tile_valid, hs, w1, w3, w2)


def _combine_kernel(x_ref, y1_ref, y2_ref, r_ref, gt_ref, *rest):
    r = r_ref[...]
    y = r[:, 2:3] * y1_ref[...].astype(F32) + r[:, 3:4] * y2_ref[...].astype(F32)
    x = x_ref[...] + gt_ref[...] * y
    if len(rest) == 2:
        g_ref, o_ref = rest
        ms = jnp.mean(x * x, axis=-1, keepdims=True)
        x = (x * lax.rsqrt(ms + NORM_EPS)) * g_ref[...]
    else:
        o_ref, = rest
    o_ref[...] = x


def _combine(x, y1, y2, r, gt_tab, g_final, *, n_rows, seq, tm):
    n_all, d = x.shape
    n_mod = gt_tab.shape[0]

    def mod_idx(i):
        return (jnp.minimum(i * tm // seq, n_mod - 1), 0, 0)

    row = lambda i: (i, 0)
    in_specs = [pl.BlockSpec((tm, d), row), pl.BlockSpec((tm, d), row), pl.BlockSpec((tm, d), row),
                pl.BlockSpec((tm, LANES), row), pl.BlockSpec((None, 1, d), mod_idx)]
    args = [x, y1, y2, r, gt_tab]
    if g_final is None:
        out_rows, aliases = n_all, {0: 0}
    else:
        out_rows, aliases = n_rows, {}
        in_specs.append(pl.BlockSpec((1, d), lambda i: (0, 0)))
        args.append(g_final)
    return pl.pallas_call(
        _combine_kernel,
        grid=(n_rows // tm,),
        in_specs=in_specs,
        out_specs=pl.BlockSpec((tm, d), row),
        out_shape=jax.ShapeDtypeStruct((out_rows, d), F32),
        input_output_aliases=aliases,
        compiler_params=_cparams("parallel"),
        name="moe_combine",
    )(*args)


def _moe(x, a_tab, sh_tab, gt_tab, w_router, w1, w3, w2, g_final, *, n_rows, seq, tm):
    n_exp = w1.shape[0]
    h, r = _router(x, a_tab, sh_tab, w_router, n_rows=n_rows, seq=seq, tm=tm)
    e = r[:, :TOP_K].astype(jnp.int32).reshape(-1)
    onehot = (e[:, None] == jnp.arange(n_exp, dtype=jnp.int32)[None, :]).astype(jnp.int32)
    csum = jnp.cumsum(onehot, axis=0)
    rank = jnp.sum(csum * onehot, axis=1) - 1
    count = csum[-1]
    padded = (count + tm - 1) // tm * tm
    seg_end = jnp.cumsum(padded)
    seg_start = seg_end - padded
    slot = seg_start[e] + rank
    n_slots = TOP_K * n_rows + n_exp * tm
    pair_token = jnp.arange(TOP_K * n_rows, dtype=jnp.int32) // TOP_K
    slot_token = jnp.zeros((n_slots,), jnp.int32).at[slot].set(pair_token, unique_indices=True)
    tile_start = jnp.arange(n_slots // tm, dtype=jnp.int32) * tm
    tile_expert = jnp.sum((tile_start[:, None] >= seg_end[None, :]).astype(jnp.int32), axis=1)
    tile_expert = jnp.minimum(tile_expert, n_exp - 1)
    tile_valid = (tile_start < seg_end[-1]).astype(jnp.int32)
    hs = jnp.take(h, slot_token, axis=0)
    ys = _moe_ffn(tile_expert, tile_valid, hs, w1, w3, w2, tm=tm)
    slot2 = slot.reshape(n_rows, TOP_K)
    y1 = jnp.take(ys, slot2[:, 0], axis=0)
    y2 = jnp.take(ys, slot2[:, 1], axis=0)
    return _combine(x, y1, y2, r, gt_tab, g_final, n_rows=n_rows, seq=seq, tm=tm)


def _rope_tables(seq, tm):
    pairs = DIFF_HEAD_DIM // 4
    t = jnp.arange(seq, dtype=jnp.int32)
    row = (t // GRID_W).astype(F32)
    col = (t % GRID_W).astype(F32)
    inv = ROPE_BASE ** (-jnp.arange(pairs, dtype=F32) / pairs)
    ang = jnp.concatenate([row[:, None] * inv, col[:, None] * inv], axis=-1)
    cos = jnp.cos(ang)
    sin = jnp.sin(ang)
    reps = LANES // DIFF_HEAD_DIM
    cos_t = jnp.tile(jnp.concatenate([cos, cos], axis=-1), (1, reps))
    sin_t = jnp.tile(jnp.concatenate([-sin, sin], axis=-1), (1, reps))
    cos_t = jnp.concatenate([cos_t, jnp.ones((tm, LANES), F32)], axis=0)
    sin_t = jnp.concatenate([sin_t, jnp.zeros((tm, LANES), F32)], axis=0)
    return cos_t, sin_t


def kernel(x, c, ctx, c_ctx, w_mod, b_mod, g_mix, g_ffn, w_in, w_out, ssm_lam_re, ssm_lam_im, ssm_b_re, ssm_b_im, ssm_c_re, ssm_c_im, ssm_log_step, ssm_d, ssm_w_glu, ssm_b_glu, diff_lam_q1, diff_lam_k1, diff_lam_q2, diff_lam_k2, diff_subln_g, na_rpb, ffn_w1, ffn_w3, ffn_w2, moe_router, moe_w1, moe_w3, moe_w2, g_final):
    batch, seq, d = x.shape
    n_ctx = ctx.shape[1]
    depth = w_mod.shape[0]
    n_lat = batch * seq
    n_all = n_lat + batch * n_ctx
    ssm_w = d // 4
    diff_w = d // 2
    na_w = d - ssm_w - diff_w
    sizes = (ssm_w, diff_w, diff_w, diff_w, na_w, na_w, na_w)
    groups = ssm_w // SSM_GROUP
    tm = 512
    assert seq % tm == 0 and (batch * n_ctx) % tm == 0 and n_ctx % S5_CHUNK == 0 and seq % S5_CHUNK == 0
    assert seq % (GRID_W * NA_KEY_ROWS) == 0

    n_mod = batch + 1
    cond = jnp.concatenate([c, c_ctx[None, :]], axis=0).astype(F32)
    pad_rows = -n_mod % SUBLANES
    cond = jnp.pad(cond, ((0, pad_rows), (0, 0)))
    mods = _adaln(cond, w_mod.astype(F32), b_mod.astype(F32))[:, :n_mod]
    mods = mods.reshape(depth, n_mod, 6, 1, d)

    perm64 = np.concatenate([np.arange(0, DIFF_HEAD_DIM, 2), np.arange(1, DIFF_HEAD_DIM, 2)])
    col = np.arange(sum(sizes))
    qk = (col >= sizes[0]) & (col < sizes[0] + 2 * diff_w)
    base = (col - sizes[0]) // DIFF_HEAD_DIM * DIFF_HEAD_DIM + sizes[0]
    col_perm = np.where(qk, base + perm64[(col - sizes[0]) % DIFF_HEAD_DIM], col)
    w_in_b = w_in[:, :, col_perm].astype(BF16)
    w_out_b = w_out.astype(BF16)
    w_glu_b = ssm_w_glu.astype(BF16)
    ffn_w1_b = ffn_w1.astype(BF16)
    ffn_w3_b = ffn_w3.astype(BF16)
    ffn_w2_b = ffn_w2.astype(BF16)
    moe_w1_b = moe_w1.astype(BF16)
    moe_w3_b = moe_w3.astype(BF16)
    moe_w2_b = moe_w2.astype(BF16)
    n_exp = moe_router.shape[-1]
    router_p = jnp.pad(moe_router.astype(F32), ((0, 0), (0, 0), (0, LANES - n_exp)))

    g_fin = g_final.astype(F32).reshape(1, d)
    cos_tab, sin_tab = _rope_tables(seq, tm)
    n_chunks_ctx = n_ctx // S5_CHUNK
    n_chunks = (seq + n_ctx) // S5_CHUNK

    xs = jnp.concatenate([x.reshape(n_lat, d), ctx.reshape(batch * n_ctx, d)], axis=0).astype(F32)

    for layer in range(depth):
        need_ctx = layer < depth - 1
        n_rows = n_all if need_ctx else n_lat
        lam_init = 0.8 - 0.6 * math.exp(-0.3 * layer)
        sh1, sc1, gt1, sh2, sc2, gt2 = (mods[layer, :, i] for i in range(6))
        a1 = g_mix[layer].astype(F32) * (1.0 + sc1)
        a2 = g_ffn[layer].astype(F32) * (1.0 + sc2)

        u, dq, dk, dv, nq, nk, nv = _in_proj(xs, a1, sh1, w_in_b[layer], cos_tab, sin_tab,
                                             n_lat=n_lat, seq=seq, sizes=sizes, tm=tm)

        u_lat = u[:n_lat].reshape(batch, seq // S5_CHUNK, S5_CHUNK, groups, SSM_GROUP)
        u_ctx = u[n_lat:].reshape(batch, n_chunks_ctx, S5_CHUNK, groups, SSM_GROUP)
        ut = jnp.concatenate([u_ctx, u_lat], axis=1)
        ut = jnp.transpose(ut, (3, 1, 0, 2, 4)).reshape(groups, n_chunks, batch, S5_CHUNK * SSM_GROUP)
        s5m = _s5_matrices(ssm_lam_re[layer], ssm_lam_im[layer], ssm_b_re[layer], ssm_b_im[layer],
                           ssm_c_re[layer], ssm_c_im[layer], ssm_log_step[layer], ssm_d[layer])
        yt = _s5(ut, *s5m, n_ctx_chunks=n_chunks_ctx)
        yt = yt.reshape(groups, n_chunks, batch, S5_CHUNK, SSM_GROUP)
        yt = jnp.transpose(yt, (2, 1, 3, 0, 4)).reshape(batch, seq + n_ctx, ssm_w)
        y_ssm = jnp.concatenate([yt[:, n_ctx:].reshape(n_lat, ssm_w), yt[:, :n_ctx].reshape(-1, ssm_w)], axis=0)

        lam = (jnp.exp(jnp.sum(diff_lam_q1[layer].astype(F32) * diff_lam_k1[layer].astype(F32)))
               - jnp.exp(jnp.sum(diff_lam_q2[layer].astype(F32) * diff_lam_k2[layer].astype(F32))) + lam_init)
        lam = lam.reshape(1, 1)
        g_sub = (diff_subln_g[layer].astype(F32) * (1.0 - lam_init)).reshape(DIFF_V_DIM, 1)
        diff = _diff_lat(lam, dq, dk, dv, g_sub, batch=batch, seq=seq, n_ctx=n_ctx, tq=256, key_chunk=1024)
        bias = _na_bias(na_rpb[layer], seq // GRID_W)
        na = _na_lat(nq, nk, nv, bias, batch=batch, seq=seq, n_ctx=n_ctx)
        if need_ctx:
            diff = _diff_ctx(lam, dq, dk, dv, g_sub, diff, batch=batch, seq=seq, n_ctx=n_ctx)
            na = _na_ctx(nq, nk, nv, na, batch=batch, seq=seq, n_ctx=n_ctx)

        xs = _out_proj(xs, y_ssm, diff, na, w_glu_b[layer], ssm_b_glu[layer].astype(F32).reshape(1, -1),
                       w_out_b[layer], gt1, n_rows=n_rows, seq=seq, tm=tm)

        i = layer // 2
        if layer % 2 == 0:
            xs = _ffn(xs, a2, sh2, gt2, ffn_w1_b[i], ffn_w3_b[i], ffn_w2_b[i], n_rows=n_rows, seq=seq, tm=tm)
        else:
            xs = _moe(xs, a2, sh2, gt2, router_p[i], moe_w1_b[i], moe_w3_b[i], moe_w2_b[i],
                      None if need_ctx else g_fin, n_rows=n_rows, seq=seq, tm=tm)

    if depth % 2 == 1:
        xs = _final_norm(xs, g_fin, n_rows=n_lat, tm=tm)
    return xs.reshape(batch, seq, d)


def _final_norm_kernel(x_ref, g_ref, o_ref):
    x = x_ref[...]
    ms = jnp.mean(x * x, axis=-1, keepdims=True)
    o_ref[...] = (x * lax.rsqrt(ms + NORM_EPS)) * g_ref[...]


def _final_norm(x, g, *, n_rows, tm):
    d = x.shape[1]
    return pl.pallas_call(
        _final_norm_kernel,
        grid=(n_rows // tm,),
        in_specs=[pl.BlockSpec((tm, d), lambda i: (i, 0)), pl.BlockSpec((1, d), lambda i: (0, 0))],
        out_specs=pl.BlockSpec((tm, d), lambda i: (i, 0)),
        out_shape=jax.ShapeDtypeStruct((n_rows, d), F32),
        compiler_params=_cparams("parallel"),
        name="final_norm",
    )(x, g)
```

```python
import functools
import math

import numpy as np
import jax
import jax.numpy as jnp
from jax import lax
from jax.experimental import pallas as pl
from jax.experimental.pallas import tpu as pltpu

F32 = jnp.float32
BF16 = jnp.bfloat16

GRID_W = 64
SSM_GROUP = 16
SSM_STATE = 64
DIFF_HEAD_DIM = 64
DIFF_V_DIM = 2 * DIFF_HEAD_DIM
NA_HEAD_DIM = 64
NA_WIN_ROWS = 8
NA_WIN_COLS = 16
ROPE_BASE = 10000.0
TOP_K = 2
NORM_EPS = 1e-6
SUBLN_EPS = 1e-5
NEG_INF = -1e30

LANES = 128
SUBLANES = 8
VMEM_LIMIT_BYTES = 56 * 1024 * 1024

MXU_DIM = 256
FF_CHUNK = MXU_DIM
ROW_DMA_UNROLL = 8
LOG2E = math.log2(math.e)
S5_CHUNK = 16
NA_BAND_ROWS = 8
NA_KEY_ROWS = 2 * NA_WIN_ROWS

_DN_T = (((1,), (1,)), ((), ()))


def _cparams(*sem):
    return pltpu.CompilerParams(dimension_semantics=sem, vmem_limit_bytes=VMEM_LIMIT_BYTES)


def _dot(a, b):
    return jnp.dot(a, b, preferred_element_type=F32)


def _dot_t(a, b):
    return lax.dot_general(a, b, _DN_T, preferred_element_type=F32)


def _split_bf16(a):
    hi = a.astype(BF16)
    lo = (a - hi.astype(F32)).astype(BF16)
    return hi, lo


def _dot3(a, b):
    a_hi, a_lo = _split_bf16(a)
    b_hi, b_lo = _split_bf16(b)
    return _dot(a_hi, b_hi) + _dot(a_hi, b_lo) + _dot(a_lo, b_hi)


def _norm_mod(x, a, sh):
    ms = jnp.mean(x * x, axis=-1, keepdims=True)
    return (x * lax.rsqrt(ms + NORM_EPS)) * a + sh


def _adaln_kernel(c_ref, w_ref, b_ref, o_ref):
    c = c_ref[...]
    s = c * jax.nn.sigmoid(c)
    o_ref[...] = _dot3(s, w_ref[...]) + b_ref[...]


def _adaln(cond, w_mod, b_mod):
    depth, d, n6 = w_mod.shape
    r = cond.shape[0]
    tn = n6 // 6
    return pl.pallas_call(
        _adaln_kernel,
        grid=(depth, n6 // tn),
        in_specs=[pl.BlockSpec((r, d), lambda l, j: (0, 0)),
                  pl.BlockSpec((None, d, tn), lambda l, j: (l, 0, j)),
                  pl.BlockSpec((None, 1, tn), lambda l, j: (l, 0, j))],
        out_specs=pl.BlockSpec((None, r, tn), lambda l, j: (l, 0, j)),
        out_shape=jax.ShapeDtypeStruct((depth, r, n6), F32),
        compiler_params=_cparams("arbitrary", "arbitrary"),
        name="adaln",
    )(cond, w_mod, b_mod.reshape(depth, 1, n6))


def _in_proj_kernel(x_ref, a_ref, sh_ref, w_ref, cos_ref, sin_ref,
                    u_ref, q_ref, k_ref, v_ref, nq_ref, nk_ref, nv_ref, *, sizes):
    h = _norm_mod(x_ref[...], a_ref[...], sh_ref[...]).astype(BF16)
    p = _dot(h, w_ref[...])
    cs = cos_ref[...]
    sn = sin_ref[...]
    lane = lax.broadcasted_iota(jnp.int32, cs.shape, 1)
    first = (lane % DIFF_HEAD_DIM) < (DIFF_HEAD_DIM // 2)

    def rope(t):
        partner = jnp.where(first, pltpu.roll(t, LANES - DIFF_HEAD_DIM // 2, 1),
                            pltpu.roll(t, DIFF_HEAD_DIM // 2, 1))
        return t * cs + partner * sn

    o = [sum(sizes[:i]) for i in range(len(sizes) + 1)]
    u_ref[...] = p[:, o[0]:o[1]].astype(BF16)
    qk_scale = DIFF_HEAD_DIM ** -0.5 * LOG2E
    for c0 in range(0, sizes[1], LANES):
        q_ref[:, c0:c0 + LANES] = (rope(p[:, o[1] + c0:o[1] + c0 + LANES]) * qk_scale).astype(BF16)
        k_ref[:, c0:c0 + LANES] = rope(p[:, o[2] + c0:o[2] + c0 + LANES]).astype(BF16)
    v_ref[...] = p[:, o[3]:o[4]].astype(BF16)
    nq_ref[...] = (p[:, o[4]:o[5]] * (NA_HEAD_DIM ** -0.5 * LOG2E)).astype(BF16)
    nk_ref[...] = p[:, o[5]:o[6]].astype(BF16)
    nv_ref[...] = p[:, o[6]:o[7]].astype(BF16)


def _in_proj(x, a_tab, sh_tab, w_in, cos_tab, sin_tab, *, n_lat, seq, sizes, tm):
    n_all, d = x.shape
    n_tiles = n_all // tm
    n_lat_tiles = n_lat // tm
    seq_tiles = seq // tm
    n_mod = a_tab.shape[0]

    def mod_idx(i):
        return (jnp.minimum(i * tm // seq, n_mod - 1), 0, 0)

    def rope_idx(i):
        return (jnp.where(i < n_lat_tiles, i % seq_tiles, seq_tiles), 0)

    row = lambda i: (i, 0)
    return pl.pallas_call(
        functools.partial(_in_proj_kernel, sizes=sizes),
        grid=(n_tiles,),
        in_specs=[pl.BlockSpec((tm, d), row),
                  pl.BlockSpec((None, 1, d), mod_idx),
                  pl.BlockSpec((None, 1, d), mod_idx),
                  pl.BlockSpec(w_in.shape, lambda i: (0, 0)),
                  pl.BlockSpec((tm, LANES), rope_idx),
                  pl.BlockSpec((tm, LANES), rope_idx)],
        out_specs=[pl.BlockSpec((tm, s), row) for s in sizes],
        out_shape=[jax.ShapeDtypeStruct((n_all, s), BF16) for s in sizes],
        compiler_params=_cparams("parallel"),
        name="in_proj",
    )(x, a_tab, sh_tab, w_in, cos_tab, sin_tab)


def _s5_kernel(u_ref, m_ref, e_ref, f_ref, a_ref, d_ref, y_ref, v_scr, s_scr, *, n_ctx_chunks):
    n_chunks, nb, width = u_ref.shape
    rows = n_chunks * nb
    half = width // 2
    u = u_ref[...].reshape(rows, width)
    y = u.astype(F32) * d_ref[...]
    for direction in range(2):
        y = y + _dot(u, m_ref[direction])
        v_scr[...] = _dot(u, e_ref[direction])
        a1 = jnp.broadcast_to(a_ref[direction, 0:1, :], (nb, half))
        a2 = jnp.broadcast_to(a_ref[direction, 1:2, :], (nb, half))
        a2s = jnp.broadcast_to(a_ref[direction, 2:3, :], (nb, half))

        def step(i, carry, direction=direction, a1=a1, a2=a2, a2s=a2s):
            s, ssw = carry
            if direction == 0:
                c = i
            else:
                c = jnp.where(i < n_ctx_chunks, n_ctx_chunks - 1 - i, n_chunks - 1 + n_ctx_chunks - i)
            r0 = pl.multiple_of(c * nb, nb)
            s_scr[pl.ds(r0, nb), :] = s
            v = v_scr[pl.ds(r0, nb), :]
            return (a1 * s + a2 * ssw + v[:, :half], a1 * ssw + a2s * s + v[:, half:])

        zero = jnp.zeros((nb, half), F32)
        lax.fori_loop(0, n_chunks, step, (zero, zero))
        y = y + _dot(s_scr[...].astype(BF16), f_ref[direction])
    y_ref[...] = y.reshape(n_chunks, nb, width).astype(BF16)


def _s5(ut, m, e, f, a, dsk, *, n_ctx_chunks):
    g, n_chunks, nb, width = ut.shape
    rows = n_chunks * nb
    return pl.pallas_call(
        functools.partial(_s5_kernel, n_ctx_chunks=n_ctx_chunks),
        grid=(g,),
        in_specs=[pl.BlockSpec((None, n_chunks, nb, width), lambda i: (i, 0, 0, 0)),
                  pl.BlockSpec((None, 2, width, width), lambda i: (i, 0, 0, 0)),
                  pl.BlockSpec((None, 2, width, width), lambda i: (i, 0, 0, 0)),
                  pl.BlockSpec((None, 2, width // 2, width), lambda i: (i, 0, 0, 0)),
                  pl.BlockSpec((None, 2, SUBLANES, width // 2), lambda i: (i, 0, 0, 0)),
                  pl.BlockSpec((None, 1, width), lambda i: (i, 0, 0))],
        out_specs=pl.BlockSpec((None, n_chunks, nb, width), lambda i: (i, 0, 0, 0)),
        out_shape=jax.ShapeDtypeStruct(ut.shape, BF16),
        scratch_shapes=[pltpu.VMEM((rows, width), F32), pltpu.VMEM((rows, width // 2), F32)],
        compiler_params=_cparams("parallel"),
        name="s5",
    )(ut, m, e, f, a, dsk)


def _s5_matrices(lam_re, lam_im, b_re, b_im, c_re, c_im, log_step, d_skip):
    t = S5_CHUNK
    dt = jnp.exp(log_step.astype(F32))[..., None]
    lr = lam_re.astype(F32)
    li = lam_im.astype(F32)
    mag = jnp.exp(lr * dt)
    ab_re = mag * jnp.cos(li * dt)
    ab_im = mag * jnp.sin(li * dt)
    den = lr * lr + li * li
    n_re = ab_re - 1.0
    n_im = ab_im
    f_re = (n_re * lr + n_im * li) / den
    f_im = (n_im * lr - n_re * li) / den
    br = b_re.astype(F32)
    bi = b_im.astype(F32)
    bb_re = f_re[..., None] * br - f_im[..., None] * bi
    bb_im = f_re[..., None] * bi + f_im[..., None] * br
    j = jnp.arange(t + 1, dtype=F32)[:, None, None, None]
    pw_mag = jnp.exp(lr * dt * j)
    pw_re = pw_mag * jnp.cos(li * dt * j)
    pw_im = pw_mag * jnp.sin(li * dt * j)
    cr = c_re.astype(F32)
    ci = c_im.astype(F32)
    cp_re = cr[None] * pw_re[:, :, :, None, :] - ci[None] * pw_im[:, :, :, None, :]
    cp_im = cr[None] * pw_im[:, :, :, None, :] + ci[None] * pw_re[:, :, :, None, :]
    taps = (jnp.einsum('jdghp,dgpk->jdghk', cp_re[:t], bb_re)
            - jnp.einsum('jdghp,dgpk->jdghk', cp_im[:t], bb_im))
    tau = np.arange(t)[:, None]
    tt = np.arange(t)[None, :]
    mats = []
    for direction, lag in enumerate((tt - tau, tau - tt)):
        valid = lag >= 0
        kk = taps[np.clip(lag, 0, t - 1), direction]
        kk = jnp.where(valid[:, :, None, None, None], kk, 0.0)
        mats.append(jnp.transpose(kk, (2, 0, 4, 1, 3)))
    m = jnp.stack(mats, axis=1)
    gcount = m.shape[0]
    hdim = SSM_GROUP
    m = m.reshape(gcount, 2, t * hdim, t * hdim)
    es = []
    for direction, power in enumerate((t - 1 - np.arange(t), np.arange(t))):
        pr = pw_re[power, direction]
        pi = pw_im[power, direction]
        e_re = pr[:, :, :, None] * bb_re[direction][None] - pi[:, :, :, None] * bb_im[direction][None]
        e_im = pr[:, :, :, None] * bb_im[direction][None] + pi[:, :, :, None] * bb_re[direction][None]
        e_re = jnp.transpose(e_re, (1, 0, 3, 2)).reshape(gcount, t * hdim, -1)
        e_im = jnp.transpose(e_im, (1, 0, 3, 2)).reshape(gcount, t * hdim, -1)
        es.append(jnp.concatenate([e_re, e_im, e_im, e_re], axis=-1))
    e = jnp.stack(es, axis=1)
    fs = []
    for direction, power in enumerate((np.arange(t) + 1, t - np.arange(t))):
        fr = cp_re[power, direction]
        fi = cp_im[power, direction]
        fr = jnp.transpose(fr, (1, 3, 0, 2)).reshape(gcount, -1, t * hdim)
        fi = jnp.transpose(fi, (1, 3, 0, 2)).reshape(gcount, -1, t * hdim)
        fs.append(jnp.concatenate([fr, -fi], axis=1))
    f = jnp.stack(fs, axis=1)
    ar = jnp.transpose(pw_re[t], (1, 0, 2))
    ai = jnp.transpose(pw_im[t], (1, 0, 2))
    rows = jnp.stack([jnp.concatenate([ar, ar], -1), jnp.concatenate([-ai, ai], -1),
                      jnp.concatenate([ai, -ai], -1)], axis=2)
    a = jnp.pad(rows, ((0, 0), (0, 0), (0, SUBLANES - 3), (0, 0)))
    dsk = jnp.tile(d_skip.astype(F32).reshape(gcount, 1, hdim), (1, 1, t))
    return m.astype(BF16), e.astype(BF16), f.astype(BF16), a, dsk


def _diff_fill(kl_ref, kc_ref, vl_ref, vc_ref, kp_scr, vt_scr):
    hd = DIFF_V_DIM
    n_keys = kp_scr.shape[0]
    n_lat = n_keys - kc_ref.shape[0]
    if kl_ref is not None:
        kp_scr[0:n_lat, 0:hd] = kl_ref[...]
        vt_scr[0:hd, 0:n_lat] = vl_ref[...].astype(F32).T.astype(BF16)
    kp_scr[n_lat:, 0:hd] = kc_ref[...]
    kp_scr[:, hd:] = jnp.zeros((n_keys, MXU_DIM - hd), BF16)
    vt_scr[0:hd, n_lat:] = vc_ref[...].astype(F32).T.astype(BF16)
    vt_scr[hd:, :] = jnp.ones((vt_scr.shape[0] - hd, n_keys), BF16)


def _diff_scores(q_ref, kp_scr, chunks):
    hd = DIFF_V_DIM
    tq = q_ref.shape[0]
    qt = q_ref[...].astype(F32).T
    row = lax.broadcasted_iota(jnp.int32, qt.shape, 0)
    q2t = jnp.concatenate([jnp.where(row < DIFF_HEAD_DIM, qt, 0.0), jnp.where(row >= DIFF_HEAD_DIM, qt, 0.0)], axis=1)
    q2t = jnp.concatenate([q2t, jnp.zeros((MXU_DIM - hd, 2 * tq), F32)], axis=0).astype(BF16)
    scores = []
    mx = None
    for c0, kn in chunks:
        s = _dot(kp_scr[c0:c0 + kn, :], q2t)
        scores.append(s)
        part = jnp.max(s.reshape(kn // SUBLANES, SUBLANES, 2 * tq), axis=0)
        mx = part if mx is None else jnp.maximum(mx, part)
    return scores, jnp.max(mx, axis=0, keepdims=True)


def _diff_output(scores, m, vt_scr, chunks, lam, g_ref, o_ref):
    hd = DIFF_V_DIM
    tq = o_ref.shape[0]
    acc = None
    for (c0, kn), s in zip(chunks, scores):
        p = jnp.exp2((s - m).astype(BF16))
        d = _dot(vt_scr[:, c0:c0 + kn], p)
        acc = d if acc is None else acc + d
    r = acc[0:hd, :] / acc[hd:hd + 1, :]
    ot = r[:, :tq] - lam * r[:, tq:]
    ms = jnp.mean(ot * ot, axis=0, keepdims=True)
    ot = ot * lax.rsqrt(ms + SUBLN_EPS) * g_ref[...]
    o_ref[...] = ot.T.astype(BF16)


def _diff_lat_kernel(lam_ref, q_ref, qn_ref, kl_ref, kc_ref, vl_ref, vc_ref, g_ref, o_ref,
                     kp_scr, vt_scr, m_a, m_b, *s_scrs, key_chunk):
    j = pl.program_id(2)
    tq = qn_ref.shape[0]
    n_keys = kp_scr.shape[0]
    chunks = _diff_chunks(n_keys - kc_ref.shape[0], kc_ref.shape[0], key_chunk)
    s_a, s_b = s_scrs[:len(chunks)], s_scrs[len(chunks):]
    lam = lam_ref[0, 0]

    def store_scores(src_ref, s_dst, m_dst):
        scores, m = _diff_scores(src_ref, kp_scr, chunks)
        for s, s_scr in zip(scores, s_dst):
            s_scr[...] = s
        m_dst[...] = m

    def finish(s_src, m_src, dst_ref):
        _diff_output([s_scr[...] for s_scr in s_src], m_src[...], vt_scr, chunks, lam, g_ref, dst_ref)

    @pl.when(j == 0)
    def _():
        _diff_fill(kl_ref, kc_ref, vl_ref, vc_ref, kp_scr, vt_scr)
        store_scores(q_ref.at[0:tq], s_a, m_a)

    store_scores(q_ref.at[tq:2 * tq], s_b, m_b)
    finish(s_a, m_a, o_ref.at[0:tq])
    store_scores(qn_ref, s_a, m_a)
    finish(s_b, m_b, o_ref.at[tq:2 * tq])


def _diff_chunks(n_lat, n_ctx, key_chunk):
    chunks = [(k0, min(key_chunk, n_lat - k0)) for k0 in range(0, n_lat, key_chunk)] if n_lat else []
    return chunks + [(n_lat, n_ctx)]


def _diff_scratch(n_lat, n_ctx):
    ones_rows = 2 * SUBLANES
    return [pltpu.VMEM((n_lat + n_ctx, MXU_DIM), BF16), pltpu.VMEM((DIFF_V_DIM + ones_rows, n_lat + n_ctx), BF16)]


def _diff_lat(lam, q, k, v, g, *, batch, seq, n_ctx, tq, key_chunk):
    n_all, width = q.shape
    heads = width // DIFF_V_DIM
    q_tiles = seq // tq
    ctx_blk0 = batch * seq // n_ctx
    smem = pl.BlockSpec(memory_space=pltpu.SMEM)
    steps = q_tiles // 2
    qspec = pl.BlockSpec((2 * tq, DIFF_V_DIM), lambda b, h, i: (b * steps + i, h))
    qnext = pl.BlockSpec((tq, DIFF_V_DIM), lambda b, h, i: (b * q_tiles + jnp.minimum(2 * i + 2, q_tiles - 1), h))
    lat = pl.BlockSpec((seq, DIFF_V_DIM), lambda b, h, i: (b, h))
    ctx = pl.BlockSpec((n_ctx, DIFF_V_DIM), lambda b, h, i: (ctx_blk0 + b, h))
    chunk_scores = [pltpu.VMEM((kn, 2 * tq), F32) for _, kn in _diff_chunks(seq, n_ctx, key_chunk)]
    pipeline_scratch = [pltpu.VMEM((1, 2 * tq), F32)] * 2 + chunk_scores * 2
    return pl.pallas_call(
        functools.partial(_diff_lat_kernel, key_chunk=key_chunk),
        grid=(batch, heads, steps),
        in_specs=[smem, qspec, qnext, lat, ctx, lat, ctx, pl.BlockSpec((DIFF_V_DIM, 1), lambda b, h, i: (0, 0))],
        out_specs=qspec,
        out_shape=jax.ShapeDtypeStruct((n_all, width), BF16),
        scratch_shapes=_diff_scratch(seq, n_ctx) + pipeline_scratch,
        compiler_params=_cparams("arbitrary", "arbitrary", "arbitrary"),
        name="diff_lat",
    )(lam, q, q, k, k, v, v, g)


def _diff_ctx_kernel(lam_ref, q_ref, kc_ref, vc_ref, g_ref, prev_ref, o_ref, kp_scr, vt_scr):
    del prev_ref
    chunks = _diff_chunks(0, kc_ref.shape[0], 0)
    _diff_fill(None, kc_ref, None, vc_ref, kp_scr, vt_scr)
    scores, m = _diff_scores(q_ref, kp_scr, chunks)
    _diff_output(scores, m, vt_scr, chunks, lam_ref[0, 0], g_ref, o_ref)


def _diff_ctx(lam, q, k, v, g, out, *, batch, seq, n_ctx):
    n_all, width = q.shape
    heads = width // DIFF_V_DIM
    ctx_blk0 = batch * seq // n_ctx
    smem = pl.BlockSpec(memory_space=pltpu.SMEM)
    ctx = pl.BlockSpec((n_ctx, DIFF_V_DIM), lambda b, h: (ctx_blk0 + b, h))
    return pl.pallas_call(
        _diff_ctx_kernel,
        grid=(batch, heads),
        in_specs=[smem, ctx, ctx, ctx, pl.BlockSpec((DIFF_V_DIM, 1), lambda b, h: (0, 0)),
                  pl.BlockSpec(memory_space=pl.ANY)],
        out_specs=ctx,
        out_shape=jax.ShapeDtypeStruct((n_all, width), BF16),
        scratch_shapes=_diff_scratch(0, n_ctx),
        input_output_aliases={5: 0},
        compiler_params=_cparams("parallel", "parallel"),
        name="diff_ctx",
    )(lam, q, k, v, g, out)


def _na_kernel(q_ref, kl_ref, vl_ref, kc_ref, vc_ref, bias_ref, o_ref, *, grid_rows):
    band = pl.program_id(0)
    key_tokens = NA_KEY_ROWS * GRID_W
    kstart = jnp.clip(band * NA_BAND_ROWS - (NA_KEY_ROWS - NA_BAND_ROWS) // 2, 0, grid_rows - NA_KEY_ROWS)
    off = pl.multiple_of(kstart * GRID_W, (NA_KEY_ROWS - NA_BAND_ROWS) // 2 * GRID_W)
    kw = kl_ref[pl.ds(off, key_tokens), :]
    vw = vl_ref[pl.ds(off, key_tokens), :]
    q = q_ref[...]
    lane = lax.broadcasted_iota(jnp.int32, q.shape, 1)
    acc = jnp.zeros(q.shape, F32)
    for h in range(q.shape[1] // NA_HEAD_DIM):
        mine = (lane >= h * NA_HEAD_DIM) & (lane < (h + 1) * NA_HEAD_DIM)
        qh = jnp.where(mine, q, jnp.zeros_like(q))
        sl = _dot_t(qh, kw) + bias_ref[h]
        sc = _dot_t(qh, kc_ref[...])
        m = jnp.maximum(jnp.max(sl, axis=-1, keepdims=True), jnp.max(sc, axis=-1, keepdims=True))
        pl_ = jnp.exp2(sl - m)
        pc = jnp.exp2(sc - m)
        l = jnp.sum(pl_, axis=-1, keepdims=True) + jnp.sum(pc, axis=-1, keepdims=True)
        o = (_dot(pl_.astype(BF16), vw) + _dot(pc.astype(BF16), vc_ref[...])) / l
        acc = jnp.where(mine, o, acc)
    o_ref[...] = acc.astype(BF16)


def _na_lat(q, k, v, bias, *, batch, seq, n_ctx):
    n_all, width = q.shape
    grid_rows = seq // GRID_W
    n_bands = grid_rows // NA_BAND_ROWS
    tq = NA_BAND_ROWS * GRID_W
    ctx_blk0 = batch * seq // n_ctx
    heads = width // NA_HEAD_DIM

    def variant(kb, b):
        return (jnp.where(kb == 0, 0, jnp.where(kb == n_bands - 1, 2, 1)), 0, 0, 0)

    qspec = pl.BlockSpec((tq, width), lambda kb, b: (b * n_bands + kb, 0))
    lat = pl.BlockSpec((seq, width), lambda kb, b: (b, 0))
    ctx = pl.BlockSpec((n_ctx, width), lambda kb, b: (ctx_blk0 + b, 0))
    return pl.pallas_call(
        functools.partial(_na_kernel, grid_rows=grid_rows),
        grid=(n_bands, batch),
        in_specs=[qspec, lat, lat, ctx, ctx,
                  pl.BlockSpec((None, heads, tq, NA_KEY_ROWS * GRID_W), variant)],
        out_specs=qspec,
        out_shape=jax.ShapeDtypeStruct((n_all, width), BF16),
        compiler_params=_cparams("arbitrary", "arbitrary"),
        name="na_lat",
    )(q, k, v, k, v, bias)


def _na_bias(rpb, grid_rows):
    n_bands = grid_rows // NA_BAND_ROWS
    qr = np.arange(NA_BAND_ROWS)[:, None]
    kr = np.arange(NA_KEY_ROWS)[None, :]
    qc = np.arange(GRID_W)[:, None]
    kc = np.arange(GRID_W)[None, :]
    cstart = np.clip(qc - NA_WIN_COLS // 2, 0, GRID_W - NA_WIN_COLS)
    col_valid = (kc >= cstart) & (kc < cstart + NA_WIN_COLS)
    col_idx = np.clip(kc - qc + NA_WIN_COLS - 1, 0, 2 * NA_WIN_COLS - 2)
    n_col = 2 * NA_WIN_COLS - 1
    col_onehot = (col_idx[None] == np.arange(n_col)[:, None, None]).astype(np.float32)
    rpb = rpb.astype(F32)
    tables = []
    for band in (0, min(1, n_bands - 1), n_bands - 1):
        r = band * NA_BAND_ROWS + qr
        start = np.clip(r - NA_WIN_ROWS // 2, 0, grid_rows - NA_WIN_ROWS)
        kstart = np.clip(band * NA_BAND_ROWS - (NA_KEY_ROWS - NA_BAND_ROWS) // 2, 0, grid_rows - NA_KEY_ROWS)
        krow = kstart + kr
        row_valid = (krow >= start) & (krow < start + NA_WIN_ROWS)
        row_idx = np.clip(krow - r + NA_WIN_ROWS - 1, 0, 2 * NA_WIN_ROWS - 2)
        by_row = rpb[:, row_idx]
        tab = jnp.einsum('hqkc,cxy->hqxky', by_row, col_onehot, precision=lax.Precision.HIGHEST)
        valid = row_valid[:, None, :, None] & col_valid[None, :, None, :]
        tab = jnp.where(valid[None], tab * LOG2E, NEG_INF)
        tables.append(tab.reshape(tab.shape[0], NA_BAND_ROWS * GRID_W, NA_KEY_ROWS * GRID_W))
    return jnp.stack(tables, axis=0)


def _na_ctx_kernel(q_ref, k_ref, v_ref, prev_ref, o_ref):
    del prev_ref
    q = q_ref[...]
    lane = lax.broadcasted_iota(jnp.int32, q.shape, 1)
    acc = jnp.zeros(q.shape, F32)
    for h in range(q.shape[1] // NA_HEAD_DIM):
        mine = (lane >= h * NA_HEAD_DIM) & (lane < (h + 1) * NA_HEAD_DIM)
        qh = jnp.where(mine, q, jnp.zeros_like(q))
        s = _dot_t(qh, k_ref[...])
        p = jnp.exp2(s - jnp.max(s, axis=-1, keepdims=True))
        o = _dot(p.astype(BF16), v_ref[...]) / jnp.sum(p, axis=-1, keepdims=True)
        acc = jnp.where(mine, o, acc)
    o_ref[...] = acc.astype(BF16)


def _na_ctx(q, k, v, out, *, batch, seq, n_ctx):
    n_all, width = q.shape
    ctx_blk0 = batch * seq // n_ctx
    ctx = pl.BlockSpec((n_ctx, width), lambda b: (ctx_blk0 + b, 0))
    return pl.pallas_call(
        _na_ctx_kernel,
        grid=(batch,),
        in_specs=[ctx, ctx, ctx, pl.BlockSpec(memory_space=pl.ANY)],
        out_specs=ctx,
        out_shape=jax.ShapeDtypeStruct((n_all, width), BF16),
        input_output_aliases={3: 0},
        compiler_params=_cparams("parallel"),
        name="na_ctx",
    )(q, k, v, out)


def _gelu_tanh(x):
    return 0.5 * x * (1.0 + jnp.tanh(math.sqrt(2.0 / math.pi) * (x + 0.044715 * (x * x * x))))


def _out_proj_kernel(x_ref, y_ref, df_ref, na_ref, wg_ref, bg_ref, wo_ref, gt_ref, o_ref):
    g = _gelu_tanh(y_ref[...].astype(F32))
    ssm = g * jax.nn.sigmoid(_dot(g.astype(BF16), wg_ref[...]) + bg_ref[...])
    w_ssm = y_ref.shape[1]
    w_diff = df_ref.shape[1]
    acc = _dot(ssm.astype(BF16), wo_ref[0:w_ssm, :])
    acc = acc + _dot(df_ref[...], wo_ref[w_ssm:w_ssm + w_diff, :])
    acc = acc + _dot(na_ref[...], wo_ref[w_ssm + w_diff:, :])
    o_ref[...] = x_ref[...] + gt_ref[...] * acc


def _out_proj(x, y_ssm, diff, na, w_glu, b_glu, w_out, gt_tab, *, n_rows, seq, tm):
    n_all, d = x.shape
    n_mod = gt_tab.shape[0]

    def mod_idx(i):
        return (jnp.minimum(i * tm // seq, n_mod - 1), 0, 0)

    row = lambda i: (i, 0)
    full = lambda i: (0, 0)
    return pl.pallas_call(
        _out_proj_kernel,
        grid=(n_rows // tm,),
        in_specs=[pl.BlockSpec((tm, d), row),
                  pl.BlockSpec((tm, y_ssm.shape[1]), row),
                  pl.BlockSpec((tm, diff.shape[1]), row),
                  pl.BlockSpec((tm, na.shape[1]), row),
                  pl.BlockSpec(w_glu.shape, full),
                  pl.BlockSpec(b_glu.shape, full),
                  pl.BlockSpec(w_out.shape, full),
                  pl.BlockSpec((None, 1, d), mod_idx)],
        out_specs=pl.BlockSpec((tm, d), row),
        out_shape=jax.ShapeDtypeStruct((n_all, d), F32),
        input_output_aliases={0: 0},
        compiler_params=_cparams("parallel"),
        name="out_proj",
    )(x, y_ssm, diff, na, w_glu, b_glu, w_out, gt_tab)


def _swiglu(h, w1_ref, w3_ref, w2_ref, act_ref):
    f_total = w1_ref.shape[1]
    tf = FF_CHUNK if f_total % FF_CHUNK == 0 else LANES
    for f0 in range(0, f_total, tf):
        a = _dot(h, w1_ref[:, f0:f0 + tf])
        b = _dot(h, w3_ref[:, f0:f0 + tf])
        act_ref[:, f0:f0 + tf] = (a * jax.nn.sigmoid(a) * b).astype(BF16)
    return _dot(act_ref[...], w2_ref[...])


def _ffn_kernel(x_ref, a_ref, sh_ref, gt_ref, w1_ref, w3_ref, w2_ref, o_ref, act_ref):
    x = x_ref[...]
    h = _norm_mod(x, a_ref[...], sh_ref[...]).astype(BF16)
    o_ref[...] = x + gt_ref[...] * _swiglu(h, w1_ref, w3_ref, w2_ref, act_ref)


def _resident(shape, index_map):
    return pl.BlockSpec(shape, index_map, pipeline_mode=pl.Buffered(1))


def _ffn(x, a_tab, sh_tab, gt_tab, w1, w3, w2, *, n_rows, seq, tm):
    n_all, d = x.shape
    n_mod = a_tab.shape[0]

    def mod_idx(i):
        return (jnp.minimum(i * tm // seq, n_mod - 1), 0, 0)

    row = lambda i: (i, 0)
    full = lambda i: (0, 0)
    mod = pl.BlockSpec((None, 1, d), mod_idx)
    return pl.pallas_call(
        _ffn_kernel,
        grid=(n_rows // tm,),
        in_specs=[pl.BlockSpec((tm, d), row), mod, mod, mod,
                  _resident(w1.shape, full), _resident(w3.shape, full), _resident(w2.shape, full)],
        out_specs=pl.BlockSpec((tm, d), row),
        out_shape=jax.ShapeDtypeStruct((n_all, d), F32),
        scratch_shapes=[pltpu.VMEM((tm, w1.shape[1]), BF16)],
        input_output_aliases={0: 0},
        compiler_params=_cparams("parallel"),
        name="ffn",
    )(x, a_tab, sh_tab, gt_tab, w1, w3, w2)


def _router_kernel(x_ref, a_ref, sh_ref, wr_ref, h_ref, r_ref):
    h = _norm_mod(x_ref[...], a_ref[...], sh_ref[...])
    h_ref[...] = h
    n_exp = 8
    logits = _dot3(h, wr_ref[...])
    lane = lax.broadcasted_iota(jnp.int32, logits.shape, 1)
    logits = jnp.where(lane < n_exp, logits, -jnp.inf)
    m1 = jnp.max(logits, axis=-1, keepdims=True)
    i1 = jnp.min(jnp.where(logits == m1, lane, LANES), axis=-1, keepdims=True)
    rest = jnp.where(lane == i1, -jnp.inf, logits)
    m2 = jnp.max(rest, axis=-1, keepdims=True)
    i2 = jnp.min(jnp.where(rest == m2, lane, LANES), axis=-1, keepdims=True)
    e2 = jnp.exp(m2 - m1)
    g1 = 1.0 / (1.0 + e2)
    g2 = e2 / (1.0 + e2)
    r_ref[...] = jnp.where(lane == 0, i1.astype(F32),
                           jnp.where(lane == 1, i2.astype(F32),
                                     jnp.where(lane == 2, g1, jnp.where(lane == 3, g2, 0.0))))


def _router(x, a_tab, sh_tab, w_router, *, n_rows, seq, tm):
    n_all, d = x.shape
    n_mod = a_tab.shape[0]

    def mod_idx(i):
        return (jnp.minimum(i * tm // seq, n_mod - 1), 0, 0)

    row = lambda i: (i, 0)
    mod = pl.BlockSpec((None, 1, d), mod_idx)
    return pl.pallas_call(
        _router_kernel,
        grid=(n_rows // tm,),
        in_specs=[pl.BlockSpec((tm, d), row), mod, mod, pl.BlockSpec(w_router.shape, lambda i: (0, 0))],
        out_specs=[pl.BlockSpec((tm, d), row), pl.BlockSpec((tm, LANES), row)],
        out_shape=[jax.ShapeDtypeStruct((n_rows, d), F32), jax.ShapeDtypeStruct((n_rows, LANES), F32)],
        compiler_params=_cparams("parallel"),
        name="router",
    )(x, a_tab, sh_tab, w_router)


def _moe_ffn_kernel(te_ref, tv_ref, h_ref, w1_ref, w3_ref, w2_ref, o_ref, act_ref):
    i = pl.program_id(0)

    @pl.when(tv_ref[i] > 0)
    def _():
        o_ref[...] = _swiglu(h_ref[...].astype(BF16), w1_ref, w3_ref, w2_ref, act_ref)

    @pl.when(tv_ref[i] == 0)
    def _():
        o_ref[...] = jnp.zeros(o_ref.shape, F32)


def _moe_ffn(tile_expert, tile_valid, hs, w1, w3, w2, *, tm):
    n_slots, d = hs.shape
    wspec = lambda w: _resident((None,) + w.shape[1:], lambda i, te, tv: (te[i], 0, 0))
    row = lambda i, te, tv: (i, 0)
    return pl.pallas_call(
        _moe_ffn_kernel,
        grid_spec=pltpu.PrefetchScalarGridSpec(
            num_scalar_prefetch=2,
            grid=(n_slots // tm,),
            in_specs=[pl.BlockSpec((tm, d), row), wspec(w1), wspec(w3), wspec(w2)],
            out_specs=pl.BlockSpec((tm, d), row),
            scratch_shapes=[pltpu.VMEM((tm, w1.shape[2]), BF16)]),
        out_shape=jax.ShapeDtypeStruct((n_slots, d), F32),
        compiler_params=_cparams("arbitrary"),
        name="moe_ffn",
    )(tile_expert, tile_valid, hs, w1, w3, w2)


def _row_copy(src_ref, src_row, dst_ref, dst_row, sem):
    return pltpu.make_async_copy(src_ref.at[pl.ds(src_row, 1)], dst_ref.at[pl.ds(dst_row, 1)], sem)


def _dispatch_kernel(slot_ref, h_ref, init_ref, hs_ref, sem):
    del init_ref
    tm = h_ref.shape[0]

    def start(r, carry):
        for k in range(TOP_K):
            _row_copy(h_ref, r, hs_ref, slot_ref[0, TOP_K * r + k], sem).start()
        return carry

    def wait(r, carry):
        for k in range(TOP_K):
            _row_copy(h_ref, 0, hs_ref, 0, sem).wait()
        return carry

    lax.fori_loop(0, tm, start, 0, unroll=ROW_DMA_UNROLL)
    lax.fori_loop(0, tm, wait, 0, unroll=ROW_DMA_UNROLL)


def _dispatch(slots, h, n_slots, *, tm):
    n_rows, d = h.shape
    return pl.pallas_call(
        _dispatch_kernel,
        grid=(n_rows // tm,),
        in_specs=[pl.BlockSpec((None, 1, TOP_K * tm), lambda i: (i, 0, 0), memory_space=pltpu.SMEM),
                  pl.BlockSpec((tm, d), lambda i: (i, 0)),
                  pl.BlockSpec(memory_space=pl.ANY)],
        out_specs=pl.BlockSpec(memory_space=pl.ANY),
        out_shape=jax.ShapeDtypeStruct((n_slots, d), F32),
        scratch_shapes=[pltpu.SemaphoreType.DMA],
        input_output_aliases={2: 0},
        compiler_params=_cparams("arbitrary"),
        name="moe_dispatch",
    )(slots, h, jnp.zeros((n_slots, d), F32))


def _combine_kernel(slot_ref, x_ref, r_ref, gt_ref, *rest):
    if len(rest) == 6:
        g_ref, ys_ref, o_ref, y_buf, sem = rest[0], rest[1], rest[2], rest[3:5], rest[5]
    else:
        g_ref, ys_ref, o_ref, y_buf, sem = None, rest[0], rest[1], rest[2:4], rest[4]
    tm = x_ref.shape[0]

    def start(r, carry):
        for k in range(TOP_K):
            _row_copy(ys_ref, slot_ref[0, TOP_K * r + k], y_buf[k], r, sem).start()
        return carry

    def wait(r, carry):
        for k in range(TOP_K):
            _row_copy(ys_ref, 0, y_buf[k], 0, sem).wait()
        return carry

    lax.fori_loop(0, tm, start, 0, unroll=ROW_DMA_UNROLL)
    lax.fori_loop(0, tm, wait, 0, unroll=ROW_DMA_UNROLL)
    r = r_ref[...]
    y = r[:, 2:3] * y_buf[0][...] + r[:, 3:4] * y_buf[1][...]
    x = x_ref[...] + gt_ref[...] * y
    if g_ref is not None:
        ms = jnp.mean(x * x, axis=-1, keepdims=True)
        x = (x * lax.rsqrt(ms + NORM_EPS)) * g_ref[...]
    o_ref[...] = x


def _combine(slots, x, ys, r, gt_tab, g_final, *, n_rows, seq, tm):
    n_all, d = x.shape
    n_mod = gt_tab.shape[0]

    def mod_idx(i):
        return (jnp.minimum(i * tm // seq, n_mod - 1), 0, 0)

    row = lambda i: (i, 0)
    in_specs = [pl.BlockSpec((None, 1, TOP_K * tm), lambda i: (i, 0, 0), memory_space=pltpu.SMEM),
                pl.BlockSpec((tm, d), row), pl.BlockSpec((tm, LANES), row), pl.BlockSpec((None, 1, d), mod_idx)]
    args = [slots, x, r, gt_tab]
    if g_final is None:
        out_rows, aliases = n_all, {1: 0}
    else:
        out_rows, aliases = n_rows, {}
        in_specs.append(pl.BlockSpec((1, d), lambda i: (0, 0)))
        args.append(g_final)
    in_specs.append(pl.BlockSpec(memory_space=pl.ANY))
    args.append(ys)
    return pl.pallas_call(
        _combine_kernel,
        grid=(n_rows // tm,),
        in_specs=in_specs,
        out_specs=pl.BlockSpec((tm, d), row),
        out_shape=jax.ShapeDtypeStruct((out_rows, d), F32),
        scratch_shapes=[pltpu.VMEM((tm, d), F32)] * TOP_K + [pltpu.SemaphoreType.DMA],
        input_output_aliases=aliases,
        compiler_params=_cparams("arbitrary"),
        name="moe_combine",
    )(*args)


def _moe(x, a_tab, sh_tab, gt_tab, w_router, w1, w3, w2, g_final, *, n_rows, seq, tm):
    n_exp = w1.shape[0]
    h, r = _router(x, a_tab, sh_tab, w_router, n_rows=n_rows, seq=seq, tm=tm)
    e = r[:, :TOP_K].astype(jnp.int32).reshape(-1)
    onehot = (e[:, None] == jnp.arange(n_exp, dtype=jnp.int32)[None, :]).astype(jnp.int32)
    csum = jnp.cumsum(onehot, axis=0)
    rank = jnp.sum(csum * onehot, axis=1) - 1
    count = csum[-1]
    padded = (count + tm - 1) // tm * tm
    seg_end = jnp.cumsum(padded)
    seg_start = seg_end - padded
    slot = seg_start[e] + rank
    n_slots = TOP_K * n_rows + n_exp * tm
    tile_start = jnp.arange(n_slots // tm, dtype=jnp.int32) * tm
    tile_expert = jnp.sum((tile_start[:, None] >= seg_end[None, :]).astype(jnp.int32), axis=1)
    tile_expert = jnp.minimum(tile_expert, n_exp - 1)
    tile_valid = (tile_start < seg_end[-1]).astype(jnp.int32)
    slots = slot.astype(jnp.int32).reshape(n_rows // tm, 1, TOP_K * tm)
    hs = _dispatch(slots, h, n_slots, tm=tm)
    ys = _moe_ffn(tile_expert, tile_valid, hs, w1, w3, w2, tm=tm)
    return _combine(slots, x, ys, r, gt_tab, g_final, n_rows=n_rows, seq=seq, tm=tm)


def _rope_tables(seq, tm):
    pairs = DIFF_HEAD_DIM // 4
    t = jnp.arange(seq, dtype=jnp.int32)
    row = (t // GRID_W).astype(F32)
    col = (t % GRID_W).astype(F32)
    inv = ROPE_BASE ** (-jnp.arange(pairs, dtype=F32) / pairs)
    ang = jnp.concatenate([row[:, None] * inv, col[:, None] * inv], axis=-1)
    cos = jnp.cos(ang)
    sin = jnp.sin(ang)
    reps = LANES // DIFF_HEAD_DIM
    cos_t = jnp.tile(jnp.concatenate([cos, cos], axis=-1), (1, reps))
    sin_t = jnp.tile(jnp.concatenate([-sin, sin], axis=-1), (1, reps))
    cos_t = jnp.concatenate([cos_t, jnp.ones((tm, LANES), F32)], axis=0)
    sin_t = jnp.concatenate([sin_t, jnp.zeros((tm, LANES), F32)], axis=0)
    return cos_t, sin_t


def kernel(x, c, ctx, c_ctx, w_mod, b_mod, g_mix, g_ffn, w_in, w_out, ssm_lam_re, ssm_lam_im, ssm_b_re, ssm_b_im, ssm_c_re, ssm_c_im, ssm_log_step, ssm_d, ssm_w_glu, ssm_b_glu, diff_lam_q1, diff_lam_k1, diff_lam_q2, diff_lam_k2, diff_subln_g, na_rpb, ffn_w1, ffn_w3, ffn_w2, moe_router, moe_w1, moe_w3, moe_w2, g_final):
    batch, seq, d = x.shape
    n_ctx = ctx.shape[1]
    depth = w_mod.shape[0]
    n_lat = batch * seq
    n_all = n_lat + batch * n_ctx
    ssm_w = d // 4
    diff_w = d // 2
    na_w = d - ssm_w - diff_w
    sizes = (ssm_w, diff_w, diff_w, diff_w, na_w, na_w, na_w)
    groups = ssm_w // SSM_GROUP
    tm = 512
    assert seq % tm == 0 and (batch * n_ctx) % tm == 0 and n_ctx % S5_CHUNK == 0 and seq % S5_CHUNK == 0
    assert seq % (GRID_W * NA_KEY_ROWS) == 0

    n_mod = batch + 1
    cond = jnp.concatenate([c, c_ctx[None, :]], axis=0).astype(F32)
    pad_rows = -n_mod % SUBLANES
    cond = jnp.pad(cond, ((0, pad_rows), (0, 0)))
    mods = _adaln(cond, w_mod.astype(F32), b_mod.astype(F32))[:, :n_mod]
    mods = mods.reshape(depth, n_mod, 6, 1, d)

    perm64 = np.concatenate([np.arange(0, DIFF_HEAD_DIM, 2), np.arange(1, DIFF_HEAD_DIM, 2)])
    col = np.arange(sum(sizes))
    qk = (col >= sizes[0]) & (col < sizes[0] + 2 * diff_w)
    base = (col - sizes[0]) // DIFF_HEAD_DIM * DIFF_HEAD_DIM + sizes[0]
    col_perm = np.where(qk, base + perm64[(col - sizes[0]) % DIFF_HEAD_DIM], col)
    w_in_b = w_in[:, :, col_perm].astype(BF16)
    w_out_b = w_out.astype(BF16)
    w_glu_b = ssm_w_glu.astype(BF16)
    ffn_w1_b = ffn_w1.astype(BF16)
    ffn_w3_b = ffn_w3.astype(BF16)
    ffn_w2_b = ffn_w2.astype(BF16)
    moe_w1_b = moe_w1.astype(BF16)
    moe_w3_b = moe_w3.astype(BF16)
    moe_w2_b = moe_w2.astype(BF16)
    n_exp = moe_router.shape[-1]
    router_p = jnp.pad(moe_router.astype(F32), ((0, 0), (0, 0), (0, LANES - n_exp)))

    g_fin = g_final.astype(F32).reshape(1, d)
    cos_tab, sin_tab = _rope_tables(seq, tm)
    n_chunks_ctx = n_ctx // S5_CHUNK
    n_chunks = (seq + n_ctx) // S5_CHUNK

    xs = jnp.concatenate([x.reshape(n_lat, d), ctx.reshape(batch * n_ctx, d)], axis=0).astype(F32)

    for layer in range(depth):
        need_ctx = layer < depth - 1
        n_rows = n_all if need_ctx else n_lat
        lam_init = 0.8 - 0.6 * math.exp(-0.3 * layer)
        sh1, sc1, gt1, sh2, sc2, gt2 = (mods[layer, :, i] for i in range(6))
        a1 = g_mix[layer].astype(F32) * (1.0 + sc1)
        a2 = g_ffn[layer].astype(F32) * (1.0 + sc2)

        u, dq, dk, dv, nq, nk, nv = _in_proj(xs, a1, sh1, w_in_b[layer], cos_tab, sin_tab,
                                             n_lat=n_lat, seq=seq, sizes=sizes, tm=tm)

        u_lat = u[:n_lat].reshape(batch, seq // S5_CHUNK, S5_CHUNK, groups, SSM_GROUP)
        u_ctx = u[n_lat:].reshape(batch, n_chunks_ctx, S5_CHUNK, groups, SSM_GROUP)
        ut = jnp.concatenate([u_ctx, u_lat], axis=1)
        ut = jnp.transpose(ut, (3, 1, 0, 2, 4)).reshape(groups, n_chunks, batch, S5_CHUNK * SSM_GROUP)
        s5m = _s5_matrices(ssm_lam_re[layer], ssm_lam_im[layer], ssm_b_re[layer], ssm_b_im[layer],
                           ssm_c_re[layer], ssm_c_im[layer], ssm_log_step[layer], ssm_d[layer])
        yt = _s5(ut, *s5m, n_ctx_chunks=n_chunks_ctx)
        yt = yt.reshape(groups, n_chunks, batch, S5_CHUNK, SSM_GROUP)
        yt = jnp.transpose(yt, (2, 1, 3, 0, 4)).reshape(batch, seq + n_ctx, ssm_w)
        y_ssm = jnp.concatenate([yt[:, n_ctx:].reshape(n_lat, ssm_w), yt[:, :n_ctx].reshape(-1, ssm_w)], axis=0)

        lam = (jnp.exp(jnp.sum(diff_lam_q1[layer].astype(F32) * diff_lam_k1[layer].astype(F32)))
               - jnp.exp(jnp.sum(diff_lam_q2[layer].astype(F32) * diff_lam_k2[layer].astype(F32))) + lam_init)
        lam = lam.reshape(1, 1)
        g_sub = (diff_subln_g[layer].astype(F32) * (1.0 - lam_init)).reshape(DIFF_V_DIM, 1)
        diff = _diff_lat(lam, dq, dk, dv, g_sub, batch=batch, seq=seq, n_ctx=n_ctx, tq=256, key_chunk=1024)
        bias = _na_bias(na_rpb[layer], seq // GRID_W)
        na = _na_lat(nq, nk, nv, bias, batch=batch, seq=seq, n_ctx=n_ctx)
        if need_ctx:
            diff = _diff_ctx(lam, dq, dk, dv, g_sub, diff, batch=batch, seq=seq, n_ctx=n_ctx)
            na = _na_ctx(nq, nk, nv, na, batch=batch, seq=seq, n_ctx=n_ctx)

        xs = _out_proj(xs, y_ssm, diff, na, w_glu_b[layer], ssm_b_glu[layer].astype(F32).reshape(1, -1),
                       w_out_b[layer], gt1, n_rows=n_rows, seq=seq, tm=tm)

        i = layer // 2
        if layer % 2 == 0:
            xs = _ffn(xs, a2, sh2, gt2, ffn_w1_b[i], ffn_w3_b[i], ffn_w2_b[i], n_rows=n_rows, seq=seq, tm=tm)
        else:
            xs = _moe(xs, a2, sh2, gt2, router_p[i], moe_w1_b[i], moe_w3_b[i], moe_w2_b[i],
                      None if need_ctx else g_fin, n_rows=n_rows, seq=seq, tm=tm)

    if depth % 2 == 1:
        xs = _final_norm(xs, g_fin, n_rows=n_lat, tm=tm)
    return xs.reshape(batch, seq, d)


def _final_norm_kernel(x_ref, g_ref, o_ref):
    x = x_ref[...]
    ms = jnp.mean(x * x, axis=-1, keepdims=True)
    o_ref[...] = (x * lax.rsqrt(ms + NORM_EPS)) * g_ref[...]


def _final_norm(x, g, *, n_rows, tm):
    d = x.shape[1]
    return pl.pallas_call(
        _final_norm_kernel,
        grid=(n_rows // tm,),
        in_specs=[pl.BlockSpec((tm, d), lambda i: (i, 0)), pl.BlockSpec((1, d), lambda i: (0, 0))],
        out_specs=pl.BlockSpec((tm, d), lambda i: (i, 0)),
        out_shape=jax.ShapeDtypeStruct((n_rows, d), F32),
        compiler_params=_cparams("parallel"),
        name="final_norm",
    )(x, g)
```
